```python
import jax, jax.numpy as jnp
from jax import lax
import numpy as np

D_MODEL = 1024
BATCH = 8
SEQ = 2048
DEPTH = 4
DEC_BATCH = 128
DEC_SEQ = 4
PAST_LEN = 2048
PAGE_SIZE = 128

N_A_LAYERS = DEPTH // 2
N_B_LAYERS = DEPTH - N_A_LAYERS
CHUNK = 128
E_A = 2 * D_MODEL
N_GROUPS_A = 8
GROUP_A = E_A // N_GROUPS_A
N_HEADS = 16
HEAD_DIM = 64
E_B = N_HEADS * HEAD_DIM
Q_BLOCK = 128
PLE_DIM = 256
SB_BIAS_INIT = -7.0
EPS = 1e-6

kernel_name = "yoco_gmlp_stickbreaking_step"


def rms_norm(x, g):
    x32 = x.astype(jnp.float32)
    y = x32 * lax.rsqrt(jnp.mean(x32 * x32, axis=-1, keepdims=True) + EPS)
    return (y * g.astype(jnp.float32)).astype(x.dtype)


def layer_norm(x, g, b):
    x32 = x.astype(jnp.float32)
    mu = jnp.mean(x32, axis=-1, keepdims=True)
    xc = x32 - mu
    y = xc * lax.rsqrt(jnp.mean(xc * xc, axis=-1, keepdims=True) + EPS)
    return (y * g.astype(jnp.float32) + b.astype(jnp.float32)).astype(x.dtype)


def chunk_gmlp(xn, w_in, ln_g, ln_b, w_s, b_s, w_out):
    b_, t, _ = xn.shape
    u, v, gate = jnp.split(xn @ w_in, 3, axis=-1)
    v = layer_norm(v, ln_g, ln_b)
    c = min(CHUNK, t)
    causal = jnp.tril(jnp.ones((c, c), dtype=bool))
    ws = jnp.where(causal[None], w_s[:, :c, :c], 0.0)
    vc = v.reshape(b_, t // c, c, N_GROUPS_A, GROUP_A)
    s = jnp.einsum('gts,bnsgc->bntgc', ws, vc) + b_s[:, :c].T[None, None, :, :, None]
    s = s.reshape(b_, t, E_A)
    y = (u * s * jax.nn.silu(gate)) @ w_out
    return y, v


def stick_breaking(q, k, v, bias, q_pos, k_pos):
    b_, t, h, dh = q.shape
    blk = min(Q_BLOCK, t)
    nb = t // blk
    scale = dh ** -0.5
    k32 = k.astype(jnp.float32)
    v32 = v.astype(jnp.float32)
    bias32 = bias.astype(jnp.float32)[None, :, None, None]
    qb = q.reshape(b_, nb, blk, h, dh).transpose(1, 0, 2, 3, 4)
    pb = q_pos.reshape(nb, blk)

    def block(args):
        qi, pi = args
        z = jnp.einsum('bqhd,bkhd->bhqk', qi.astype(jnp.float32), k32) * scale + bias32
        mask = (k_pos[None, :] < pi[:, None])[None, None]
        log_1mb = jnp.where(mask, jax.nn.log_sigmoid(-z), 0.0)
        after = lax.cumsum(log_1mb, axis=3, reverse=True) - log_1mb
        w = jnp.where(mask, jnp.exp(jax.nn.log_sigmoid(z) + after), 0.0)
        return jnp.einsum('bhqk,bkhd->bqhd', w, v32)

    o = lax.map(block, (qb, pb))
    return o.transpose(1, 0, 2, 3, 4).reshape(b_, t, h, dh).astype(q.dtype)


def stick_branch(xn, w_in, w_out, bias, k_all, v_all, q_pos, k_pos):
    b_, t, _ = xn.shape
    q, gate = jnp.split(xn @ w_in, 2, axis=-1)
    o = stick_breaking(q.reshape(b_, t, N_HEADS, HEAD_DIM), k_all, v_all, bias, q_pos, k_pos)
    return (o.reshape(b_, t, E_B) * jax.nn.silu(gate)) @ w_out


def per_layer_embed(h, p_i, w_ple_i, g_ple_i, w_gate_i):
    gate = jax.nn.sigmoid(rms_norm(h, g_ple_i) @ w_gate_i)
    return h + (p_i @ w_ple_i) * gate


def trunk(x, p, past_k, past_v, keep_chunk_state, g_norm, w_in_a, ln_v_g, ln_v_b, w_spatial, b_spatial,
          w_out_a, g_kv, w_kv, w_in_b, w_out_b, b_sb, w_ple, g_ple, w_ple_gate, g_final):
    b_, t, _ = x.shape
    start = 0 if past_k is None else past_k.shape[1]
    h = x
    chunk_rows = []
    k_new = v_new = k_all = v_all = None
    q_pos = start + jnp.arange(t)
    k_pos = None
    for i in range(DEPTH):
        xn = rms_norm(h, g_norm[i])
        if i < N_A_LAYERS:
            y, v_rows = chunk_gmlp(xn, w_in_a[i], ln_v_g[i], ln_v_b[i], w_spatial[i], b_spatial[i], w_out_a[i])
            if keep_chunk_state:
                chunk_rows.append(v_rows)
        else:
            j = i - N_A_LAYERS
            y = stick_branch(xn, w_in_b[j], w_out_b[j], b_sb[j], k_all, v_all, q_pos, k_pos)
        h = h + y
        h = per_layer_embed(h, p[i], w_ple[i], g_ple[i], w_ple_gate[i])
        if i == N_A_LAYERS - 1:
            k_flat, v_flat = jnp.split(rms_norm(h, g_kv) @ w_kv, 2, axis=-1)
            k_new = k_flat.reshape(b_, t, N_HEADS, HEAD_DIM)
            v_new = v_flat.reshape(b_, t, N_HEADS, HEAD_DIM)
            if past_k is None:
                k_all, v_all = k_new, v_new
            else:
                k_all = jnp.concatenate([past_k, k_new], axis=1)
                v_all = jnp.concatenate([past_v, v_new], axis=1)
            k_pos = jnp.arange(k_all.shape[1])
    y_out = rms_norm(h, g_final)
    chunk_state = jnp.stack(chunk_rows) if keep_chunk_state else None
    return y_out, k_new, v_new, chunk_state


def setup_inputs(seed: int = 0) -> dict:
    key = jax.random.key(seed)
    ks = jax.random.split(key, 24)
    f32 = jnp.float32
    n_pages = PAST_LEN // PAGE_SIZE
    n_used = DEC_BATCH * n_pages
    n_pool = n_used + n_used // 4
    nrm = lambda k, shape, s: (jax.random.normal(k, shape, f32) * s).astype(f32)
    page_table = jax.random.permutation(ks[4], n_pool)[:n_used].reshape(DEC_BATCH, n_pages).astype(jnp.int32)
    return {
        "x_prompt": nrm(ks[0], (BATCH, SEQ, D_MODEL), 1.0),
        "x_sample": nrm(ks[1], (DEC_BATCH, DEC_SEQ, D_MODEL), 1.0),
        "cache_k": nrm(ks[2], (n_pool, PAGE_SIZE, N_HEADS, HEAD_DIM), 1.0),
        "cache_v": nrm(ks[3], (n_pool, PAGE_SIZE, N_HEADS, HEAD_DIM), 1.0),
        "page_table": page_table,
        "p_prompt": nrm(ks[5], (DEPTH, BATCH, SEQ, PLE_DIM), 1.0),
        "p_sample": nrm(ks[6], (DEPTH, DEC_BATCH, DEC_SEQ, PLE_DIM), 1.0),
        "g_norm": 1.0 + nrm(ks[7], (DEPTH, D_MODEL), 0.02),
        "w_in_a": nrm(ks[8], (N_A_LAYERS, D_MODEL, 3 * E_A), D_MODEL ** -0.5),
        "ln_v_g": 1.0 + nrm(ks[9], (N_A_LAYERS, E_A), 0.02),
        "ln_v_b": nrm(ks[10], (N_A_LAYERS, E_A), 0.02),
        "w_spatial": nrm(ks[11], (N_A_LAYERS, N_GROUPS_A, CHUNK, CHUNK), CHUNK ** -0.5),
        "b_spatial": nrm(ks[12], (N_A_LAYERS, N_GROUPS_A, CHUNK), 0.02),
        "w_out_a": nrm(ks[13], (N_A_LAYERS, E_A, D_MODEL), 0.5 * E_A ** -0.5),
        "g_kv": 1.0 + nrm(ks[14], (D_MODEL,), 0.02),
        "w_kv": nrm(ks[15], (D_MODEL, 2 * E_B), D_MODEL ** -0.5),
        "w_in_b": nrm(ks[16], (N_B_LAYERS, D_MODEL, 2 * E_B), D_MODEL ** -0.5),
        "w_out_b": nrm(ks[17], (N_B_LAYERS, E_B, D_MODEL), E_B ** -0.5),
        "b_sb": SB_BIAS_INIT + nrm(ks[22], (N_B_LAYERS, N_HEADS), 0.1),
        "w_ple": nrm(ks[18], (DEPTH, PLE_DIM, D_MODEL), 0.5 * PLE_DIM ** -0.5),
        "g_ple": 1.0 + nrm(ks[19], (DEPTH, D_MODEL), 0.02),
        "w_ple_gate": nrm(ks[20], (DEPTH, D_MODEL, D_MODEL), D_MODEL ** -0.5),
        "g_final": 1.0 + nrm(ks[21], (D_MODEL,), 0.02),
    }


def reference(x_prompt, x_sample, cache_k, cache_v, page_table, p_prompt, p_sample, g_norm, w_in_a, ln_v_g,
              ln_v_b, w_spatial, b_spatial, w_out_a, g_kv, w_kv, w_in_b, w_out_b, b_sb, w_ple, g_ple, w_ple_gate,
              g_final):
    weights = (g_norm, w_in_a, ln_v_g, ln_v_b, w_spatial, b_spatial, w_out_a, g_kv, w_kv, w_in_b, w_out_b,
               b_sb, w_ple, g_ple, w_ple_gate, g_final)
    y_prompt, k_prompt, v_prompt, _ = trunk(x_prompt, p_prompt, None, None, False, *weights)
    n_seq, n_pages = page_table.shape
    past_k = cache_k[page_table].reshape(n_seq, n_pages * PAGE_SIZE, N_HEADS, HEAD_DIM)
    past_v = cache_v[page_table].reshape(n_seq, n_pages * PAGE_SIZE, N_HEADS, HEAD_DIM)
    y_sample, k_sample, v_sample, chunk_v_sample = trunk(x_sample, p_sample, past_k, past_v, True, *weights)
    return (y_prompt, y_sample, k_prompt, v_prompt, k_sample, v_sample, chunk_v_sample)
```

```python
import functools

import jax
import jax.numpy as jnp
from jax import lax
from jax.experimental import pallas as pl
from jax.experimental.pallas import tpu as pltpu

D_MODEL = 1024
DEPTH = 4
N_A_LAYERS = DEPTH // 2
CHUNK = 128
E_A = 2 * D_MODEL
N_GROUPS_A = 8
GROUP_A = E_A // N_GROUPS_A
N_HEADS = 16
HEAD_DIM = 64
E_B = N_HEADS * HEAD_DIM
PLE_DIM = 256
PAGE_SIZE = 128
EPS = 1e-6

LANES = 128
HEADS_PER_LANE_TILE = LANES // HEAD_DIM
KEY_BLOCK = 128
ROW_TILE = 256
PAGES_PER_STEP = 4
VMEM_LIMIT_BYTES = 56 * 1024 * 1024

F32 = jnp.float32
BF16 = jnp.bfloat16


def _dot(a, b):
    return jnp.dot(a, b, preferred_element_type=F32)


def _dot_nt(a, b):
    return lax.dot_general(a, b, (((1,), (1,)), ((), ())), preferred_element_type=F32)


def _rms(x, g):
    return x * lax.rsqrt(jnp.mean(x * x, axis=-1, keepdims=True) + EPS) * g


def _sigmoid(x):
    return 1.0 / (1.0 + jnp.exp(-x))


def _silu(x):
    return x * _sigmoid(x)


def _const_spec(shape):
    zeros = (0,) * len(shape)
    return pl.BlockSpec(shape, lambda *_: zeros, pipeline_mode=pl.Buffered(1))


def _ple(h1, p, wple_ref, gple_ref, wgate_ref):
    gate = _sigmoid(_dot(_rms(h1, gple_ref[...]).astype(BF16), wgate_ref[...]))
    return h1 + _dot(p.astype(BF16), wple_ref[...]) * gate


def _a_layer_kernel(*refs, chunk, with_vn, with_kv):
    (h_ref, p_ref, gn_ref, win_ref, lng_ref, lnb_ref, ws_ref, bs_ref, wout_ref,
     wple_ref, gple_ref, wgate_ref) = refs[:12]
    pos = 12
    if with_kv:
        gkv_ref, wkv_ref = refs[pos:pos + 2]
        pos += 2
    h_out = refs[pos]
    pos += 1
    if with_vn:
        vn_out = refs[pos]
        pos += 1
    if with_kv:
        k_out, v_out, kb_out, vb_out = refs[pos:pos + 4]
        pos += 4
    vn_scr, m_scr = refs[pos:pos + 2]

    x = h_ref[...]
    xn = _rms(x, gn_ref[...]).astype(BF16)

    v = _dot(xn, win_ref[:, E_A:2 * E_A])
    mu = jnp.mean(v, axis=-1, keepdims=True)
    vc = v - mu
    vn = vc * lax.rsqrt(jnp.mean(vc * vc, axis=-1, keepdims=True) + EPS) * lng_ref[...] + lnb_ref[...]
    if with_vn:
        vn_out[...] = vn
    vn_scr[...] = vn.astype(BF16)

    row = lax.broadcasted_iota(jnp.int32, (CHUNK, CHUNK), 0)
    col = lax.broadcasted_iota(jnp.int32, (CHUNK, CHUNK), 1)
    shift = chunk.bit_length() - 1
    mix_mask = ((row >> shift) == (col >> shift)) & (col <= row)

    rows = x.shape[0]
    for g in range(N_GROUPS_A):
        lo, hi = g * GROUP_A, (g + 1) * GROUP_A
        wsm = jnp.where(mix_mask, ws_ref[g], 0.0).astype(BF16)
        bias = bs_ref[g]
        bias = jnp.concatenate([bias] * (GROUP_A // LANES), axis=1)
        u = _dot(xn, win_ref[:, lo:hi])
        gt = _dot(xn, win_ref[:, 2 * E_A + lo:2 * E_A + hi])
        ug = u * _silu(gt)
        for c in range(rows // CHUNK):
            r0, r1 = c * CHUNK, (c + 1) * CHUNK
            s = _dot(wsm, vn_scr[r0:r1, lo:hi]) + bias
            m_scr[r0:r1, lo:hi] = (ug[r0:r1] * s).astype(BF16)

    h1 = x + _dot(m_scr[...], wout_ref[...])
    h2 = _ple(h1, p_ref[...], wple_ref, gple_ref, wgate_ref)
    h_out[...] = h2
    if with_kv:
        kv = _dot(_rms(h2, gkv_ref[...]).astype(BF16), wkv_ref[...])
        k = kv[:, :E_B]
        vv = kv[:, E_B:]
        k_out[...] = k
        v_out[...] = vv
        kb_out[...] = k.astype(BF16)
        vb_out[...] = vv.astype(BF16)


def _a_layer(h, p, gn, win, lng, lnb, ws, bs, wout, wple, gple, wgate, gkv=None, wkv=None, *,
             chunk, with_vn):
    rows = h.shape[0]
    with_kv = gkv is not None
    tm = ROW_TILE
    row_spec = lambda width: pl.BlockSpec((tm, width), lambda i: (i, 0))
    args = [h, p, gn, win, lng, lnb, ws, bs, wout, wple, gple, wgate]
    in_specs = [row_spec(D_MODEL), row_spec(PLE_DIM)] + [_const_spec(a.shape) for a in args[2:]]
    if with_kv:
        args += [gkv, wkv]
        in_specs += [_const_spec(gkv.shape), _const_spec(wkv.shape)]
    out_shape = [jax.ShapeDtypeStruct((rows, D_MODEL), F32)]
    out_specs = [row_spec(D_MODEL)]
    if with_vn:
        out_shape.append(jax.ShapeDtypeStruct((rows, E_A), F32))
        out_specs.append(row_spec(E_A))
    if with_kv:
        out_shape += [jax.ShapeDtypeStruct((rows, E_B), F32)] * 2 + [jax.ShapeDtypeStruct((rows, E_B), BF16)] * 2
        out_specs += [row_spec(E_B)] * 4
    return pl.pallas_call(
        functools.partial(_a_layer_kernel, chunk=chunk, with_vn=with_vn, with_kv=with_kv),
        grid=(rows // tm,),
        in_specs=in_specs,
        out_specs=out_specs,
        out_shape=out_shape,
        scratch_shapes=[pltpu.VMEM((tm, E_A), BF16), pltpu.VMEM((tm, E_A), BF16)],
        compiler_params=pltpu.CompilerParams(dimension_semantics=("arbitrary",),
                                             vmem_limit_bytes=VMEM_LIMIT_BYTES),
        name="a_layer",
    )(*args)


def _b_pre_kernel(h_ref, gn_ref, win_ref, q_out, sg_out):
    xn = _rms(h_ref[...], gn_ref[...]).astype(BF16)
    q = _dot(xn, win_ref[:, :E_B]) * (HEAD_DIM ** -0.5)
    q_out[...] = q.astype(q_out.dtype)
    sg_out[...] = _silu(_dot(xn, win_ref[:, E_B:]))


def _b_pre(h, gn, win, q_dtype):
    rows = h.shape[0]
    tm = ROW_TILE
    row_spec = pl.BlockSpec((tm, D_MODEL), lambda i: (i, 0))
    return pl.pallas_call(
        _b_pre_kernel,
        grid=(rows // tm,),
        in_specs=[row_spec, _const_spec(gn.shape), _const_spec(win.shape)],
        out_specs=[row_spec, row_spec],
        out_shape=[jax.ShapeDtypeStruct((rows, E_B), q_dtype), jax.ShapeDtypeStruct((rows, E_B), F32)],
        compiler_params=pltpu.CompilerParams(dimension_semantics=("arbitrary",),
                                             vmem_limit_bytes=VMEM_LIMIT_BYTES),
        name="b_pre",
    )(h, gn, win)


def _b_post_kernel(h_ref, m_ref, p_ref, wout_ref, wple_ref, gple_ref, wgate_ref, gfin_ref, h_out, *,
                   final_norm):
    h1 = h_ref[...] + _dot(m_ref[...].astype(BF16), wout_ref[...])
    h2 = _ple(h1, p_ref[...], wple_ref, gple_ref, wgate_ref)
    if final_norm:
        h2 = _rms(h2, gfin_ref[...])
    h_out[...] = h2


def _b_post(h, m, p, wout, wple, gple, wgate, gfin, *, final_norm):
    rows = h.shape[0]
    tm = ROW_TILE
    row_spec = lambda width: pl.BlockSpec((tm, width), lambda i: (i, 0))
    consts = [wout, wple, gple, wgate, gfin]
    return pl.pallas_call(
        functools.partial(_b_post_kernel, final_norm=final_norm),
        grid=(rows // tm,),
        in_specs=[row_spec(D_MODEL), row_spec(E_B), row_spec(PLE_DIM)] + [_const_spec(a.shape) for a in consts],
        out_specs=row_spec(D_MODEL),
        out_shape=jax.ShapeDtypeStruct((rows, D_MODEL), F32),
        compiler_params=pltpu.CompilerParams(dimension_semantics=("arbitrary",),
                                             vmem_limit_bytes=VMEM_LIMIT_BYTES),
        name="b_post",
    )(h, m, p, *consts)


def _sb_tile(z, mask, cum_ref, carry):
    sp = jnp.maximum(z, 0.0) + jnp.log(1.0 + jnp.exp(-jnp.abs(z)))
    log_1mb = -sp
    if mask is not None:
        log_1mb = jnp.where(mask, log_1mb, 0.0)
    hi = log_1mb.astype(BF16)
    lo = (log_1mb - hi.astype(F32)).astype(BF16)
    cs = _dot(jnp.concatenate([hi, lo], axis=1), cum_ref[...])
    after = cs[:, :KEY_BLOCK] + carry
    w = jnp.exp(z - sp + after)
    if mask is not None:
        w = jnp.where(mask, w, 0.0)
    return w, cs[:, KEY_BLOCK:]


def _cum_matrix():
    j = jnp.arange(KEY_BLOCK)[:, None]
    s = jnp.arange(KEY_BLOCK)[None, :]
    strict = (j > s).astype(BF16)
    half = jnp.concatenate([strict, jnp.ones((KEY_BLOCK, LANES), BF16)], axis=1)
    return jnp.concatenate([half, half], axis=0)


def _attn_prompt_kernel(bias_ref, q_ref, k_ref, v_ref, sg_ref, cum_ref, o_ref, acc_scr, carry_scr):
    hp = pl.program_id(1)
    qi = pl.program_id(2)
    tq = q_ref.shape[0]

    q2 = q_ref[...]
    lane = lax.broadcasted_iota(jnp.int32, (tq, LANES), 1)
    first = lane < HEAD_DIM
    zero = jnp.zeros_like(q2)
    qs = jnp.concatenate([jnp.where(first, q2, zero), jnp.where(first, zero, q2)], axis=0)

    row = lax.broadcasted_iota(jnp.int32, (2 * tq, KEY_BLOCK), 0)
    key = lax.broadcasted_iota(jnp.int32, (2 * tq, KEY_BLOCK), 1)
    bias = jnp.where(row < tq, bias_ref[HEADS_PER_LANE_TILE * hp], bias_ref[HEADS_PER_LANE_TILE * hp + 1])
    diag_mask = key < jnp.where(row < tq, row, row - tq)

    acc_scr[...] = jnp.zeros_like(acc_scr)
    carry_scr[...] = jnp.zeros_like(carry_scr)

    def tile(j, mask):
        start = pl.multiple_of(j * KEY_BLOCK, KEY_BLOCK)
        z = _dot_nt(qs, k_ref[pl.ds(start, KEY_BLOCK), :]) + bias
        w, inc = _sb_tile(z, mask, cum_ref, carry_scr[...])
        acc_scr[...] += _dot(w.astype(BF16), v_ref[pl.ds(start, KEY_BLOCK), :])
        carry_scr[...] += inc

    tile(qi, diag_mask)

    def body(i, c):
        tile(qi - 1 - i, None)
        return c

    lax.fori_loop(0, qi, body, 0)

    acc = acc_scr[...]
    o2 = jnp.where(first, acc[:tq], acc[tq:])
    o_ref[...] = (o2 * sg_ref[...]).astype(o_ref.dtype)


def _attn_prompt(q, kb, vb, sg, bias, cum, *, batch, seq):
    tq = KEY_BLOCK
    nq = seq // tq
    n_tiles = E_B // LANES
    q_spec = pl.BlockSpec((tq, LANES), lambda b, hp, qi: (b * nq + qi, hp))
    kv_spec = pl.BlockSpec((seq, LANES), lambda b, hp, qi: (b, hp))
    return pl.pallas_call(
        _attn_prompt_kernel,
        grid=(batch, n_tiles, nq),
        in_specs=[pl.BlockSpec(memory_space=pltpu.SMEM), q_spec, kv_spec, kv_spec, q_spec,
                  _const_spec(cum.shape)],
        out_specs=q_spec,
        out_shape=jax.ShapeDtypeStruct((batch * seq, E_B), BF16),
        scratch_shapes=[pltpu.VMEM((2 * tq, LANES), F32), pltpu.VMEM((2 * tq, KEY_BLOCK), F32)],
        compiler_params=pltpu.CompilerParams(dimension_semantics=("arbitrary",) * 3,
                                             vmem_limit_bytes=VMEM_LIMIT_BYTES),
        name="attn_prompt",
    )(bias, q, kb, vb, sg, cum)


def _attn_sample_kernel(*refs, dec_seq, n_steps):
    pt_ref, q_ref, sg_ref, bias_ref, kn_ref, vn_ref, cum_ref = refs[:7]
    k_refs = refs[7:7 + PAGES_PER_STEP]
    v_refs = refs[7 + PAGES_PER_STEP:7 + 2 * PAGES_PER_STEP]
    o_ref, q_scr, acc_scr, carry_scr, pad_scr = refs[7 + 2 * PAGES_PER_STEP:]
    del pt_ref
    step = pl.program_id(1)
    n_rows = dec_seq * N_HEADS

    head = lax.broadcasted_iota(jnp.int32, (N_HEADS, E_B), 0)
    feat = lax.broadcasted_iota(jnp.int32, (N_HEADS, E_B), 1)
    head_mask = (feat >> (HEAD_DIM.bit_length() - 1)) == head

    def tile(k, v, mask, keys_on_lanes):
        k = k.astype(BF16)
        v = v.astype(BF16)
        z = (_dot(q_scr[...], k) if keys_on_lanes else _dot_nt(q_scr[...], k)) + bias_ref[...]
        w, inc = _sb_tile(z, mask, cum_ref, carry_scr[...])
        w = w.astype(BF16)
        acc_scr[...] += _dot_nt(w, v) if keys_on_lanes else _dot(w, v)
        carry_scr[...] += inc

    @pl.when(step == 0)
    def _():
        q = q_ref[0]
        for t in range(dec_seq):
            rep = jnp.broadcast_to(q[t:t + 1, :], (N_HEADS, E_B))
            q_scr[t * N_HEADS:(t + 1) * N_HEADS, :] = jnp.where(head_mask, rep, 0.0).astype(BF16)
        acc_scr[...] = jnp.zeros_like(acc_scr)
        carry_scr[...] = jnp.zeros_like(carry_scr)
        row = lax.broadcasted_iota(jnp.int32, (n_rows, KEY_BLOCK), 0)
        key = lax.broadcasted_iota(jnp.int32, (n_rows, KEY_BLOCK), 1)
        new_mask = key < (row >> (N_HEADS.bit_length() - 1))
        pad_scr[...] = jnp.zeros_like(pad_scr)
        pad_scr[0:dec_seq, :] = kn_ref[0]
        k_new = pad_scr[...]
        pad_scr[0:dec_seq, :] = vn_ref[0]
        tile(k_new, pad_scr[...], new_mask, False)

    for u in range(PAGES_PER_STEP):
        tile(k_refs[u][0], v_refs[u][0], None, True)

    @pl.when(step == n_steps - 1)
    def _():
        sg = sg_ref[0]
        for t in range(dec_seq):
            blk = jnp.where(head_mask, acc_scr[t * N_HEADS:(t + 1) * N_HEADS, :], 0.0)
            o_ref[0, t:t + 1, :] = jnp.sum(blk, axis=0, keepdims=True) * sg[t:t + 1, :]


def _attn_sample(q, sg, bias_rows, k_new, v_new, cache_k, cache_v, page_table, cum):
    n_seq, dec_seq, _ = q.shape
    n_pages = page_table.shape[1]
    n_steps = n_pages // PAGES_PER_STEP
    n_rows = dec_seq * N_HEADS
    seq_spec = pl.BlockSpec((1, dec_seq, E_B), lambda b, j, pt: (b, 0, 0))

    def page_spec(u):
        return pl.BlockSpec((1, E_B, PAGE_SIZE),
                            lambda b, j, pt: (pt[b, n_pages - 1 - (j * PAGES_PER_STEP + u)], 0, 0))

    const = lambda a: pl.BlockSpec(a.shape, lambda b, j, pt: (0,) * a.ndim, pipeline_mode=pl.Buffered(1))
    page_specs = [page_spec(u) for u in range(PAGES_PER_STEP)]
    grid_spec = pltpu.PrefetchScalarGridSpec(
        num_scalar_prefetch=1,
        grid=(n_seq, n_steps),
        in_specs=[seq_spec, seq_spec, const(bias_rows), seq_spec, seq_spec, const(cum)] + page_specs + page_specs,
        out_specs=seq_spec,
        scratch_shapes=[pltpu.VMEM((n_rows, E_B), BF16), pltpu.VMEM((n_rows, E_B), F32),
                        pltpu.VMEM((n_rows, KEY_BLOCK), F32), pltpu.VMEM((KEY_BLOCK, E_B), F32)],
    )
    return pl.pallas_call(
        functools.partial(_attn_sample_kernel, dec_seq=dec_seq, n_steps=n_steps),
        grid_spec=grid_spec,
        out_shape=jax.ShapeDtypeStruct((n_seq, dec_seq, E_B), F32),
        compiler_params=pltpu.CompilerParams(dimension_semantics=("arbitrary", "arbitrary"),
                                             vmem_limit_bytes=VMEM_LIMIT_BYTES),
        name="attn_sample",
    )(page_table, q, sg, bias_rows, k_new, v_new, cum, *([cache_k] * PAGES_PER_STEP),
      *([cache_v] * PAGES_PER_STEP))


def _row(a):
    return a.reshape(1, -1)


def _trunk(x, p, wts, attend, *, chunk, keep_chunk_state):
    (g_norm, w_in_a, ln_v_g, ln_v_b, ws_tiles, bs_tiles, w_out_a, g_kv, w_kv, w_in_b, w_out_b,
     w_ple, g_ple, w_ple_gate, g_final) = wts
    h = x
    chunk_rows = []
    k = v = kb = vb = None
    for i in range(N_A_LAYERS):
        last = i == N_A_LAYERS - 1
        outs = _a_layer(h, p[i], _row(g_norm[i]), w_in_a[i], _row(ln_v_g[i]), _row(ln_v_b[i]), ws_tiles[i],
                        bs_tiles[i], w_out_a[i], w_ple[i], _row(g_ple[i]), w_ple_gate[i],
                        _row(g_kv) if last else None, w_kv if last else None,
                        chunk=chunk, with_vn=keep_chunk_state)
        h = outs[0]
        pos = 1
        if keep_chunk_state:
            chunk_rows.append(outs[pos])
            pos += 1
        if last:
            k, v, kb, vb = outs[pos:pos + 4]
    for j in range(DEPTH - N_A_LAYERS):
        i = N_A_LAYERS + j
        q, sg = attend.pre(h, _row(g_norm[i]), w_in_b[j])
        m = attend(j, q, sg, k, v, kb, vb)
        h = _b_post(h, m, p[i], w_out_b[j], w_ple[i], _row(g_ple[i]), w_ple_gate[i], _row(g_final),
                    final_norm=(i == DEPTH - 1))
    return h, k, v, chunk_rows


class _PromptAttend:
    def __init__(self, batch, seq, b_sb, cum):
        self.batch, self.seq, self.b_sb, self.cum = batch, seq, b_sb, cum

    def pre(self, h, gn, win):
        return _b_pre(h, gn, win, BF16)

    def __call__(self, j, q, sg, k, v, kb, vb):
        return _attn_prompt(q, kb, vb, sg, self.b_sb[j], self.cum, batch=self.batch, seq=self.seq)


class _SampleAttend:
    def __init__(self, n_seq, dec_seq, b_sb, cum, cache_k, cache_v, page_table):
        self.n_seq, self.dec_seq, self.b_sb, self.cum = n_seq, dec_seq, b_sb, cum
        self.cache_k, self.cache_v, self.page_table = cache_k, cache_v, page_table

    def pre(self, h, gn, win):
        return _b_pre(h, gn, win, F32)

    def __call__(self, j, q, sg, k, v, kb, vb):
        shape = (self.n_seq, self.dec_seq, E_B)
        bias_rows = jnp.broadcast_to(jnp.tile(self.b_sb[j], self.dec_seq)[:, None],
                                     (self.dec_seq * N_HEADS, KEY_BLOCK))
        m = _attn_sample(q.reshape(shape), sg.reshape(shape), bias_rows, k.reshape(shape), v.reshape(shape),
                         self.cache_k, self.cache_v, self.page_table, self.cum)
        return m.reshape(self.n_seq * self.dec_seq, E_B)


def kernel(x_prompt, x_sample, cache_k, cache_v, page_table, p_prompt, p_sample, g_norm, w_in_a, ln_v_g, ln_v_b,
           w_spatial, b_spatial, w_out_a, g_kv, w_kv, w_in_b, w_out_b, b_sb, w_ple, g_ple, w_ple_gate, g_final):
    batch, seq, _ = x_prompt.shape
    n_seq, dec_seq, _ = x_sample.shape
    n_pool = cache_k.shape[0]
    assert seq % ROW_TILE == 0 and (n_seq * dec_seq) % ROW_TILE == 0
    assert CHUNK % dec_seq == 0 and dec_seq & (dec_seq - 1) == 0

    bf = lambda a: a.astype(BF16)
    cum = _cum_matrix()
    pages_t = lambda c: jnp.transpose(c, (0, 2, 3, 1)).reshape(n_pool, E_B, PAGE_SIZE)

    def weights(chunk):
        rep = CHUNK // chunk
        ws_tiles = jnp.tile(w_spatial[:, :, :chunk, :chunk], (1, 1, rep, rep))
        bs_tiles = jnp.broadcast_to(jnp.tile(b_spatial[:, :, :chunk], (1, 1, rep))[..., None],
                                    (N_A_LAYERS, N_GROUPS_A, CHUNK, LANES))
        return (g_norm, bf(w_in_a), ln_v_g, ln_v_b, ws_tiles, bs_tiles, bf(w_out_a), g_kv, bf(w_kv), bf(w_in_b),
                bf(w_out_b), bf(w_ple), g_ple, bf(w_ple_gate), g_final)

    y_p, k_p, v_p, _ = _trunk(
        x_prompt.reshape(batch * seq, D_MODEL), p_prompt.reshape(DEPTH, batch * seq, PLE_DIM), weights(CHUNK),
        _PromptAttend(batch, seq, b_sb, cum), chunk=CHUNK, keep_chunk_state=False)

    y_s, k_s, v_s, chunk_rows = _trunk(
        x_sample.reshape(n_seq * dec_seq, D_MODEL), p_sample.reshape(DEPTH, n_seq * dec_seq, PLE_DIM),
        weights(dec_seq),
        _SampleAttend(n_seq, dec_seq, b_sb, cum, pages_t(cache_k), pages_t(cache_v), page_table),
        chunk=dec_seq, keep_chunk_state=True)

    return (y_p.reshape(batch, seq, D_MODEL),
            y_s.reshape(n_seq, dec_seq, D_MODEL),
            k_p.reshape(batch, seq, N_HEADS, HEAD_DIM),
            v_p.reshape(batch, seq, N_HEADS, HEAD_DIM),
            k_s.reshape(n_seq, dec_seq, N_HEADS, HEAD_DIM),
            v_s.reshape(n_seq, dec_seq, N_HEADS, HEAD_DIM),
            jnp.stack(chunk_rows).reshape(N_A_LAYERS, n_seq, dec_seq, E_A))
```

```python
import functools

import jax
import jax.numpy as jnp
import numpy as np
from jax import lax
from jax.experimental import pallas as pl
from jax.experimental.pallas import tpu as pltpu

D_MODEL = 1024
DEPTH = 4
N_A_LAYERS = DEPTH // 2
CHUNK = 128
E_A = 2 * D_MODEL
N_GROUPS_A = 8
GROUP_A = E_A // N_GROUPS_A
N_HEADS = 16
HEAD_DIM = 64
E_B = N_HEADS * HEAD_DIM
PLE_DIM = 256
PAGE_SIZE = 128
EPS = 1e-6

LANES = 128
HEADS_PER_LANE_TILE = LANES // HEAD_DIM
KEY_BLOCK = 128
KEY_UNIT = 2 * KEY_BLOCK
Q_SUB = 128
Q_SUPER = 512
ROW_CHUNK = 32
SCRATCH_SETS = 2
ROW_TILE = 256
PAGES_PER_STEP = 8
VMEM_LIMIT_BYTES = 56 * 1024 * 1024

F32 = jnp.float32
BF16 = jnp.bfloat16
SIGN_BIT = np.int32(-2 ** 31)


def _dot(a, b):
    return jnp.dot(a, b, preferred_element_type=F32)


def _dot_nt(a, b):
    return lax.dot_general(a, b, (((1,), (1,)), ((), ())), preferred_element_type=F32)


def _rms(x, g):
    return x * lax.rsqrt(jnp.mean(x * x, axis=-1, keepdims=True) + EPS) * g


def _sigmoid(x):
    return 1.0 / (1.0 + jnp.exp(-x))


def _silu(x):
    return x * _sigmoid(x)


def _const_spec(shape):
    zeros = (0,) * len(shape)
    return pl.BlockSpec(shape, lambda *_: zeros, pipeline_mode=pl.Buffered(1))


def _ple(h1, p, wple_ref, gple_ref, wgate_ref):
    gate = _sigmoid(_dot(_rms(h1, gple_ref[...]).astype(BF16), wgate_ref[...]))
    return h1 + _dot(p.astype(BF16), wple_ref[...]) * gate


def _a_layer_kernel(*refs, chunk, with_vn, with_kv, kv_transposed):
    (h_ref, p_ref, gn_ref, win_ref, lng_ref, lnb_ref, ws_ref, bs_ref, wout_ref,
     wple_ref, gple_ref, wgate_ref) = refs[:12]
    pos = 12
    if with_kv:
        gkv_ref, wkv_ref = refs[pos:pos + 2]
        pos += 2
    h_out = refs[pos]
    pos += 1
    if with_vn:
        vn_out = refs[pos]
        pos += 1
    if with_kv:
        k_out, v_out, kb_out, vb_out = refs[pos:pos + 4]
        pos += 4
    vn_scr, m_scr = refs[pos:pos + 2]

    x = h_ref[...]
    xn = _rms(x, gn_ref[...]).astype(BF16)

    v = _dot(xn, win_ref[:, E_A:2 * E_A])
    mu = jnp.mean(v, axis=-1, keepdims=True)
    vc = v - mu
    vn = vc * lax.rsqrt(jnp.mean(vc * vc, axis=-1, keepdims=True) + EPS) * lng_ref[...] + lnb_ref[...]
    if with_vn:
        vn_out[...] = vn
    vn_scr[...] = vn.astype(BF16)

    row = lax.broadcasted_iota(jnp.int32, (CHUNK, CHUNK), 0)
    col = lax.broadcasted_iota(jnp.int32, (CHUNK, CHUNK), 1)
    shift = chunk.bit_length() - 1
    mix_mask = ((row >> shift) == (col >> shift)) & (col <= row)

    rows = x.shape[0]
    for g in range(N_GROUPS_A):
        lo, hi = g * GROUP_A, (g + 1) * GROUP_A
        wsm = jnp.where(mix_mask, ws_ref[g], 0.0).astype(BF16)
        bias = bs_ref[g]
        bias = jnp.concatenate([bias] * (GROUP_A // LANES), axis=1)
        u = _dot(xn, win_ref[:, lo:hi])
        gt = _dot(xn, win_ref[:, 2 * E_A + lo:2 * E_A + hi])
        ug = u * _silu(gt)
        for c in range(rows // CHUNK):
            r0, r1 = c * CHUNK, (c + 1) * CHUNK
            s = _dot(wsm, vn_scr[r0:r1, lo:hi]) + bias
            m_scr[r0:r1, lo:hi] = (ug[r0:r1] * s).astype(BF16)

    h1 = x + _dot(m_scr[...], wout_ref[...])
    h2 = _ple(h1, p_ref[...], wple_ref, gple_ref, wgate_ref)
    h_out[...] = h2
    if with_kv:
        kv = _dot(_rms(h2, gkv_ref[...]).astype(BF16), wkv_ref[...])
        k = kv[:, :E_B]
        vv = kv[:, E_B:]
        if kv_transposed:
            k_out[0] = k.T
            v_out[0] = vv.T
        else:
            k_out[...] = k
            v_out[...] = vv
        kb_out[...] = k.astype(BF16)
        vb_out[...] = vv.astype(BF16)


def _a_layer(h, p, gn, win, lng, lnb, ws, bs, wout, wple, gple, wgate, gkv=None, wkv=None, *,
             chunk, with_vn, kv_seq=None):
    rows = h.shape[0]
    with_kv = gkv is not None
    tm = ROW_TILE
    row_spec = lambda width: pl.BlockSpec((tm, width), lambda i: (i, 0))
    args = [h, p, gn, win, lng, lnb, ws, bs, wout, wple, gple, wgate]
    in_specs = [row_spec(D_MODEL), row_spec(PLE_DIM)] + [_const_spec(a.shape) for a in args[2:]]
    if with_kv:
        args += [gkv, wkv]
        in_specs += [_const_spec(gkv.shape), _const_spec(wkv.shape)]
    out_shape = [jax.ShapeDtypeStruct((rows, D_MODEL), F32)]
    out_specs = [row_spec(D_MODEL)]
    if with_vn:
        out_shape.append(jax.ShapeDtypeStruct((rows, E_A), F32))
        out_specs.append(row_spec(E_A))
    if with_kv:
        if kv_seq is None:
            out_shape += [jax.ShapeDtypeStruct((rows, E_B), F32)] * 2
            out_specs += [row_spec(E_B)] * 2
        else:
            per_seq = kv_seq // tm
            out_shape += [jax.ShapeDtypeStruct((rows // kv_seq, E_B, kv_seq), F32)] * 2
            out_specs += [pl.BlockSpec((1, E_B, tm), lambda i: (i // per_seq, 0, i % per_seq))] * 2
        out_shape += [jax.ShapeDtypeStruct((rows, E_B), BF16)] * 2
        out_specs += [row_spec(E_B)] * 2
    return pl.pallas_call(
        functools.partial(_a_layer_kernel, chunk=chunk, with_vn=with_vn, with_kv=with_kv,
                          kv_transposed=kv_seq is not None),
        grid=(rows // tm,),
        in_specs=in_specs,
        out_specs=out_specs,
        out_shape=out_shape,
        scratch_shapes=[pltpu.VMEM((tm, E_A), BF16), pltpu.VMEM((tm, E_A), BF16)],
        compiler_params=pltpu.CompilerParams(dimension_semantics=("arbitrary",),
                                             vmem_limit_bytes=VMEM_LIMIT_BYTES),
        name="a_layer",
    )(*args)


def _b_pre_kernel(h_ref, gn_ref, win_ref, q_out, sg_out):
    xn = _rms(h_ref[...], gn_ref[...]).astype(BF16)
    q = _dot(xn, win_ref[:, :E_B]) * (HEAD_DIM ** -0.5)
    q_out[...] = q.astype(q_out.dtype)
    sg_out[...] = _silu(_dot(xn, win_ref[:, E_B:]))


def _b_pre(h, gn, win, q_dtype):
    rows = h.shape[0]
    tm = ROW_TILE
    row_spec = pl.BlockSpec((tm, D_MODEL), lambda i: (i, 0))
    return pl.pallas_call(
        _b_pre_kernel,
        grid=(rows // tm,),
        in_specs=[row_spec, _const_spec(gn.shape), _const_spec(win.shape)],
        out_specs=[row_spec, row_spec],
        out_shape=[jax.ShapeDtypeStruct((rows, E_B), q_dtype), jax.ShapeDtypeStruct((rows, E_B), F32)],
        compiler_params=pltpu.CompilerParams(dimension_semantics=("arbitrary",),
                                             vmem_limit_bytes=VMEM_LIMIT_BYTES),
        name="b_pre",
    )(h, gn, win)


def _b_post_kernel(h_ref, m_ref, p_ref, wout_ref, wple_ref, gple_ref, wgate_ref, gfin_ref, h_out, *,
                   final_norm):
    h1 = h_ref[...] + _dot(m_ref[...].astype(BF16), wout_ref[...])
    h2 = _ple(h1, p_ref[...], wple_ref, gple_ref, wgate_ref)
    if final_norm:
        h2 = _rms(h2, gfin_ref[...])
    h_out[...] = h2


def _b_post(h, m, p, wout, wple, gple, wgate, gfin, *, final_norm):
    rows = h.shape[0]
    tm = ROW_TILE
    row_spec = lambda width: pl.BlockSpec((tm, width), lambda i: (i, 0))
    consts = [wout, wple, gple, wgate, gfin]
    return pl.pallas_call(
        functools.partial(_b_post_kernel, final_norm=final_norm),
        grid=(rows // tm,),
        in_specs=[row_spec(D_MODEL), row_spec(E_B), row_spec(PLE_DIM)] + [_const_spec(a.shape) for a in consts],
        out_specs=row_spec(D_MODEL),
        out_shape=jax.ShapeDtypeStruct((rows, D_MODEL), F32),
        compiler_params=pltpu.CompilerParams(dimension_semantics=("arbitrary",),
                                             vmem_limit_bytes=VMEM_LIMIT_BYTES),
        name="b_post",
    )(h, m, p, *consts)


def _sb_tile(z, mask, cum_ref, carry):
    sp = jnp.maximum(z, 0.0) + jnp.log(1.0 + jnp.exp(-jnp.abs(z)))
    log_1mb = -sp
    if mask is not None:
        log_1mb = jnp.where(mask, log_1mb, 0.0)
    hi = log_1mb.astype(BF16)
    lo = (log_1mb - hi.astype(F32)).astype(BF16)
    cs = _dot(jnp.concatenate([hi, lo], axis=1), cum_ref[...])
    after = cs[:, :KEY_BLOCK] + carry
    w = jnp.exp(z - sp + after)
    if mask is not None:
        w = jnp.where(mask, w, 0.0)
    return w, cs[:, KEY_BLOCK:]


def _cum_matrix():
    j = jnp.arange(KEY_BLOCK)[:, None]
    s = jnp.arange(KEY_BLOCK)[None, :]
    strict = (j > s).astype(BF16)
    half = jnp.concatenate([strict, jnp.ones((KEY_BLOCK, LANES), BF16)], axis=1)
    return jnp.concatenate([half, half], axis=0)


def _attn_prompt_kernel(bias_ref, q_ref, k_ref, v_ref, sg_ref, cum_ref, o_ref,
                        qs_scr, z_scr, lcat_scr, lsz_scr, cs_scr, w_scr, acc_scr, carry_scr):
    hp = pl.program_id(1)
    qi = pl.program_id(2)
    n_sub = Q_SUPER // Q_SUB
    m_rows = HEADS_PER_LANE_TILE * Q_SUPER
    units_per_super = Q_SUPER // KEY_UNIT
    halves = KEY_UNIT // KEY_BLOCK

    first = lax.broadcasted_iota(jnp.int32, (Q_SUB, LANES), 1) < HEAD_DIM
    for a in range(n_sub):
        q2 = q_ref[a * Q_SUB:(a + 1) * Q_SUB, :]
        zero = jnp.zeros_like(q2)
        qs_scr[(2 * a) * Q_SUB:(2 * a + 1) * Q_SUB, :] = jnp.where(first, q2, zero)
        qs_scr[(2 * a + 1) * Q_SUB:(2 * a + 2) * Q_SUB, :] = jnp.where(first, zero, q2)
    acc_scr[...] = jnp.zeros_like(acc_scr)
    carry_scr[...] = jnp.zeros_like(carry_scr)
    head_bias = [bias_ref[HEADS_PER_LANE_TILE * hp + e] for e in range(HEADS_PER_LANE_TILE)]

    def unit(ku, r0, diag, st):
        start = pl.multiple_of(ku * KEY_UNIT, KEY_UNIT)
        z_scr[st, r0:, :] = _dot_nt(qs_scr[r0:, :], k_ref[pl.ds(start, KEY_UNIT), :])
        chunks = range(r0 // ROW_CHUNK, m_rows // ROW_CHUNK)

        def visible(c):
            a, t0 = (c * ROW_CHUNK) // (2 * Q_SUB), (c * ROW_CHUNK) % Q_SUB
            t = lax.broadcasted_iota(jnp.int32, (ROW_CHUNK, KEY_UNIT), 0) + (a * Q_SUB + t0)
            s = lax.broadcasted_iota(jnp.int32, (ROW_CHUNK, KEY_UNIT), 1) + diag * KEY_UNIT
            return s < t

        for c in chunks:
            rows = slice(c * ROW_CHUNK, (c + 1) * ROW_CHUNK)
            z = z_scr[st, rows, :] + head_bias[(c * ROW_CHUNK // Q_SUB) % 2]
            neg_abs = lax.bitcast_convert_type(lax.bitcast_convert_type(z, jnp.int32) | SIGN_BIT, F32)
            sp = jnp.maximum(z, 0.0) + jnp.log(1.0 + jnp.exp(neg_abs))
            lsz_scr[st, rows, :] = z - sp
            if diag is not None:
                sp = jnp.where(visible(c), sp, 0.0)
            hi = sp.astype(BF16)
            lo = (sp - hi.astype(F32)).astype(BF16)
            for h in range(halves):
                cols = slice(h * KEY_BLOCK, (h + 1) * KEY_BLOCK)
                lcat_scr[st, h, rows, 0:KEY_BLOCK] = hi[:, cols]
                lcat_scr[st, h, rows, KEY_BLOCK:2 * KEY_BLOCK] = lo[:, cols]
        for h in range(halves):
            cs_scr[st, h, r0:, :] = _dot(lcat_scr[st, h, r0:, :], cum_ref[...])
        for c in chunks:
            rows = slice(c * ROW_CHUNK, (c + 1) * ROW_CHUNK)
            carry = carry_scr[rows, :]
            lsz = lsz_scr[st, rows, :]
            ws = [None] * halves
            for h in reversed(range(halves)):
                cs = cs_scr[st, h, rows, :]
                ws[h] = jnp.exp(lsz[:, h * KEY_BLOCK:(h + 1) * KEY_BLOCK] - (cs[:, :KEY_BLOCK] + carry))
                carry = carry + cs[:, KEY_BLOCK:]
            carry_scr[rows, :] = carry
            w = jnp.concatenate(ws, axis=1)
            if diag is not None:
                w = jnp.where(visible(c), w, 0.0)
            w_scr[st, rows, :] = w.astype(BF16)
        acc_scr[r0:, :] += _dot(w_scr[st, r0:, :], v_ref[pl.ds(start, KEY_UNIT), :])

    for d in reversed(range(units_per_super)):
        unit(qi * units_per_super + d, d * (KEY_UNIT // Q_SUB) * 2 * Q_SUB, d, d % SCRATCH_SETS)

    def body(i, c):
        for u in range(SCRATCH_SETS):
            unit((qi - i) * units_per_super - 1 - u, 0, None, u)
        return c

    lax.fori_loop(0, qi * (units_per_super // SCRATCH_SETS), body, 0)

    for a in range(n_sub):
        o2 = jnp.where(first, acc_scr[(2 * a) * Q_SUB:(2 * a + 1) * Q_SUB, :],
                       acc_scr[(2 * a + 1) * Q_SUB:(2 * a + 2) * Q_SUB, :])
        o_ref[a * Q_SUB:(a + 1) * Q_SUB, :] = (o2 * sg_ref[a * Q_SUB:(a + 1) * Q_SUB, :]).astype(o_ref.dtype)


def _attn_prompt(q, kb, vb, sg, bias, cum, *, batch, seq):
    nq = seq // Q_SUPER
    n_tiles = E_B // LANES
    m_rows = HEADS_PER_LANE_TILE * Q_SUPER
    halves = KEY_UNIT // KEY_BLOCK
    q_spec = pl.BlockSpec((Q_SUPER, LANES), lambda b, hp, qi: (b * nq + qi, hp))
    kv_spec = pl.BlockSpec((seq, LANES), lambda b, hp, qi: (b, hp))
    return pl.pallas_call(
        _attn_prompt_kernel,
        grid=(batch, n_tiles, nq),
        in_specs=[pl.BlockSpec(memory_space=pltpu.SMEM), q_spec, kv_spec, kv_spec, q_spec,
                  _const_spec(cum.shape)],
        out_specs=q_spec,
        out_shape=jax.ShapeDtypeStruct((batch * seq, E_B), BF16),
        scratch_shapes=[pltpu.VMEM((m_rows, LANES), BF16),
                        pltpu.VMEM((SCRATCH_SETS, m_rows, KEY_UNIT), F32),
                        pltpu.VMEM((SCRATCH_SETS, halves, m_rows, 2 * KEY_BLOCK), BF16),
                        pltpu.VMEM((SCRATCH_SETS, m_rows, KEY_UNIT), F32),
                        pltpu.VMEM((SCRATCH_SETS, halves, m_rows, KEY_BLOCK + LANES), F32),
                        pltpu.VMEM((SCRATCH_SETS, m_rows, KEY_UNIT), BF16),
                        pltpu.VMEM((m_rows, LANES), F32),
                        pltpu.VMEM((m_rows, LANES), F32)],
        compiler_params=pltpu.CompilerParams(dimension_semantics=("arbitrary",) * 3,
                                             vmem_limit_bytes=VMEM_LIMIT_BYTES),
        name="attn_prompt",
    )(bias, q, kb, vb, sg, cum)


def _attn_sample_kernel(*refs, dec_seq, n_steps):
    pt_ref, q_ref, sg_ref, bias_ref, kn_ref, vn_ref, cum_ref = refs[:7]
    k_refs = refs[7:7 + PAGES_PER_STEP]
    v_refs = refs[7 + PAGES_PER_STEP:7 + 2 * PAGES_PER_STEP]
    o_ref, q_scr, acc_scr, carry_scr, pad_scr = refs[7 + 2 * PAGES_PER_STEP:]
    del pt_ref
    step = pl.program_id(1)
    n_rows = dec_seq * N_HEADS

    head = lax.broadcasted_iota(jnp.int32, (N_HEADS, E_B), 0)
    feat = lax.broadcasted_iota(jnp.int32, (N_HEADS, E_B), 1)
    head_mask = (feat >> (HEAD_DIM.bit_length() - 1)) == head

    @pl.when(step == 0)
    def _():
        q = q_ref[0]
        for t in range(dec_seq):
            rep = jnp.broadcast_to(q[t:t + 1, :], (N_HEADS, E_B))
            q_scr[t * N_HEADS:(t + 1) * N_HEADS, :] = jnp.where(head_mask, rep, 0.0).astype(BF16)
        row = lax.broadcasted_iota(jnp.int32, (n_rows, KEY_BLOCK), 0)
        key = lax.broadcasted_iota(jnp.int32, (n_rows, KEY_BLOCK), 1)
        new_mask = key < (row >> (N_HEADS.bit_length() - 1))
        pad_scr[...] = jnp.zeros_like(pad_scr)
        pad_scr[0:dec_seq, :] = kn_ref[0]
        z = _dot_nt(q_scr[...], pad_scr[...].astype(BF16)) + bias_ref[...]
        w, inc = _sb_tile(z, new_mask, cum_ref, jnp.zeros((n_rows, KEY_BLOCK), F32))
        pad_scr[0:dec_seq, :] = vn_ref[0]
        acc_scr[...] = _dot(w.astype(BF16), pad_scr[...].astype(BF16))
        carry_scr[...] = inc

    q_rep = q_scr[...]
    zs, sps, parts = [], [], []
    for u in range(PAGES_PER_STEP):
        z = _dot(q_rep, k_refs[u][0].astype(BF16)) + bias_ref[...]
        sp = jnp.maximum(z, 0.0) + jnp.log(1.0 + jnp.exp(-jnp.abs(z)))
        log_1mb = -sp
        hi = log_1mb.astype(BF16)
        lo = (log_1mb - hi.astype(F32)).astype(BF16)
        zs.append(z)
        sps.append(sp)
        parts.append(jnp.concatenate([hi, lo], axis=1))
    cs = _dot(jnp.concatenate(parts, axis=0), cum_ref[...])
    carry = carry_scr[...]
    acc = acc_scr[...]
    for u in range(PAGES_PER_STEP):
        cs_u = cs[u * n_rows:(u + 1) * n_rows]
        w = jnp.exp(zs[u] - sps[u] + (cs_u[:, :KEY_BLOCK] + carry))
        acc = acc + _dot_nt(w.astype(BF16), v_refs[u][0].astype(BF16))
        carry = carry + cs_u[:, KEY_BLOCK:]
    carry_scr[...] = carry
    acc_scr[...] = acc

    @pl.when(step == n_steps - 1)
    def _():
        sg = sg_ref[0]
        for t in range(dec_seq):
            blk = jnp.where(head_mask, acc_scr[t * N_HEADS:(t + 1) * N_HEADS, :], 0.0)
            o_ref[0, t:t + 1, :] = jnp.sum(blk, axis=0, keepdims=True) * sg[t:t + 1, :]


def _attn_sample(q, sg, bias_rows, k_new, v_new, cache_k, cache_v, page_table, cum):
    n_seq, dec_seq, _ = q.shape
    n_pages = page_table.shape[1]
    n_steps = n_pages // PAGES_PER_STEP
    n_rows = dec_seq * N_HEADS
    seq_spec = pl.BlockSpec((1, dec_seq, E_B), lambda b, j, pt: (b, 0, 0))

    def page_spec(u):
        return pl.BlockSpec((1, E_B, PAGE_SIZE),
                            lambda b, j, pt: (pt[b, n_pages - 1 - (j * PAGES_PER_STEP + u)], 0, 0))

    const = lambda a: pl.BlockSpec(a.shape, lambda b, j, pt: (0,) * a.ndim, pipeline_mode=pl.Buffered(1))
    page_specs = [page_spec(u) for u in range(PAGES_PER_STEP)]
    grid_spec = pltpu.PrefetchScalarGridSpec(
        num_scalar_prefetch=1,
        grid=(n_seq, n_steps),
        in_specs=[seq_spec, seq_spec, const(bias_rows), seq_spec, seq_spec, const(cum)] + page_specs + page_specs,
        out_specs=seq_spec,
        scratch_shapes=[pltpu.VMEM((n_rows, E_B), BF16), pltpu.VMEM((n_rows, E_B), F32),
                        pltpu.VMEM((n_rows, KEY_BLOCK), F32), pltpu.VMEM((KEY_BLOCK, E_B), F32)],
    )
    return pl.pallas_call(
        functools.partial(_attn_sample_kernel, dec_seq=dec_seq, n_steps=n_steps),
        grid_spec=grid_spec,
        out_shape=jax.ShapeDtypeStruct((n_seq, dec_seq, E_B), F32),
        compiler_params=pltpu.CompilerParams(dimension_semantics=("arbitrary", "arbitrary"),
                                             vmem_limit_bytes=VMEM_LIMIT_BYTES),
        name="attn_sample",
    )(page_table, q, sg, bias_rows, k_new, v_new, cum, *([cache_k] * PAGES_PER_STEP),
      *([cache_v] * PAGES_PER_STEP))


def _row(a):
    return a.reshape(1, -1)


def _trunk(x, p, wts, attend, *, chunk, keep_chunk_state, kv_seq=None):
    (g_norm, w_in_a, ln_v_g, ln_v_b, ws_tiles, bs_tiles, w_out_a, g_kv, w_kv, w_in_b, w_out_b,
     w_ple, g_ple, w_ple_gate, g_final) = wts
    h = x
    chunk_rows = []
    k = v = kb = vb = None
    for i in range(N_A_LAYERS):
        last = i == N_A_LAYERS - 1
        outs = _a_layer(h, p[i], _row(g_norm[i]), w_in_a[i], _row(ln_v_g[i]), _row(ln_v_b[i]), ws_tiles[i],
                        bs_tiles[i], w_out_a[i], w_ple[i], _row(g_ple[i]), w_ple_gate[i],
                        _row(g_kv) if last else None, w_kv if last else None,
                        chunk=chunk, with_vn=keep_chunk_state, kv_seq=kv_seq if last else None)
        h = outs[0]
        pos = 1
        if keep_chunk_state:
            chunk_rows.append(outs[pos])
            pos += 1
        if last:
            k, v, kb, vb = outs[pos:pos + 4]
    for j in range(DEPTH - N_A_LAYERS):
        i = N_A_LAYERS + j
        q, sg = attend.pre(h, _row(g_norm[i]), w_in_b[j])
        m = attend(j, q, sg, k, v, kb, vb)
        h = _b_post(h, m, p[i], w_out_b[j], w_ple[i], _row(g_ple[i]), w_ple_gate[i], _row(g_final),
                    final_norm=(i == DEPTH - 1))
    return h, k, v, chunk_rows


class _PromptAttend:
    def __init__(self, batch, seq, b_sb, cum):
        self.batch, self.seq, self.b_sb, self.cum = batch, seq, b_sb, cum

    def pre(self, h, gn, win):
        return _b_pre(h, gn, win, BF16)

    def __call__(self, j, q, sg, k, v, kb, vb):
        return _attn_prompt(q, kb, vb, sg, self.b_sb[j], self.cum, batch=self.batch, seq=self.seq)


class _SampleAttend:
    def __init__(self, n_seq, dec_seq, b_sb, cum, cache_k, cache_v, page_table):
        self.n_seq, self.dec_seq, self.b_sb, self.cum = n_seq, dec_seq, b_sb, cum
        self.cache_k, self.cache_v, self.page_table = cache_k, cache_v, page_table

    def pre(self, h, gn, win):
        return _b_pre(h, gn, win, F32)

    def __call__(self, j, q, sg, k, v, kb, vb):
        shape = (self.n_seq, self.dec_seq, E_B)
        bias_rows = jnp.broadcast_to(jnp.tile(self.b_sb[j], self.dec_seq)[:, None],
                                     (self.dec_seq * N_HEADS, KEY_BLOCK))
        m = _attn_sample(q.reshape(shape), sg.reshape(shape), bias_rows, k.reshape(shape), v.reshape(shape),
                         self.cache_k, self.cache_v, self.page_table, self.cum)
        return m.reshape(self.n_seq * self.dec_seq, E_B)


def kernel(x_prompt, x_sample, cache_k, cache_v, page_table, p_prompt, p_sample, g_norm, w_in_a, ln_v_g, ln_v_b,
           w_spatial, b_spatial, w_out_a, g_kv, w_kv, w_in_b, w_out_b, b_sb, w_ple, g_ple, w_ple_gate, g_final):
    batch, seq, _ = x_prompt.shape
    n_seq, dec_seq, _ = x_sample.shape
    n_pool = cache_k.shape[0]
    assert seq % ROW_TILE == 0 and (n_seq * dec_seq) % ROW_TILE == 0 and seq % Q_SUPER == 0
    assert (Q_SUPER // KEY_UNIT) % SCRATCH_SETS == 0 and page_table.shape[1] % PAGES_PER_STEP == 0
    assert CHUNK % dec_seq == 0 and dec_seq & (dec_seq - 1) == 0

    bf = lambda a: a.astype(BF16)
    cum = _cum_matrix()
    pages_t = lambda c: jnp.transpose(c, (0, 2, 3, 1)).reshape(n_pool, E_B, PAGE_SIZE)

    def weights(chunk):
        rep = CHUNK // chunk
        ws_tiles = jnp.tile(w_spatial[:, :, :chunk, :chunk], (1, 1, rep, rep))
        bs_tiles = jnp.broadcast_to(jnp.tile(b_spatial[:, :, :chunk], (1, 1, rep))[..., None],
                                    (N_A_LAYERS, N_GROUPS_A, CHUNK, LANES))
        return (g_norm, bf(w_in_a), ln_v_g, ln_v_b, ws_tiles, bs_tiles, bf(w_out_a), g_kv, bf(w_kv), bf(w_in_b),
                bf(w_out_b), bf(w_ple), g_ple, bf(w_ple_gate), g_final)

    y_p, k_p, v_p, _ = _trunk(
        x_prompt.reshape(batch * seq, D_MODEL), p_prompt.reshape(DEPTH, batch * seq, PLE_DIM), weights(CHUNK),
        _PromptAttend(batch, seq, b_sb, cum), chunk=CHUNK, keep_chunk_state=False, kv_seq=seq)
    rows_last = lambda a: jnp.transpose(a.reshape(batch, N_HEADS, HEAD_DIM, seq), (0, 3, 1, 2))

    y_s, k_s, v_s, chunk_rows = _trunk(
        x_sample.reshape(n_seq * dec_seq, D_MODEL), p_sample.reshape(DEPTH, n_seq * dec_seq, PLE_DIM),
        weights(dec_seq),
        _SampleAttend(n_seq, dec_seq, b_sb, cum, pages_t(cache_k), pages_t(cache_v), page_table),
        chunk=dec_seq, keep_chunk_state=True)

    return (y_p.reshape(batch, seq, D_MODEL),
            y_s.reshape(n_seq, dec_seq, D_MODEL),
            rows_last(k_p),
            rows_last(v_p),
            k_s.reshape(n_seq, dec_seq, N_HEADS, HEAD_DIM),
            v_s.reshape(n_seq, dec_seq, N_HEADS, HEAD_DIM),
            jnp.stack(chunk_rows).reshape(N_A_LAYERS, n_seq, dec_seq, E_A))
```

```python
import functools
import math
from typing import NamedTuple

import jax
import jax.numpy as jnp
import numpy as np
from jax import lax
from jax.experimental import pallas as pl
from jax.experimental.pallas import tpu as pltpu

D_MODEL = 1024
DEPTH = 4
N_A_LAYERS = DEPTH // 2
CHUNK = 128
E_A = 2 * D_MODEL
N_GROUPS_A = 8
GROUP_A = E_A // N_GROUPS_A
N_HEADS = 16
HEAD_DIM = 64
E_B = N_HEADS * HEAD_DIM
PLE_DIM = 256
PAGE_SIZE = 128
EPS = 1e-6

LANES = 128
HEADS_PER_LANE_TILE = LANES // HEAD_DIM
KEY_BLOCK = 128
KEY_UNIT = 2 * KEY_BLOCK
Q_SUB = 128
Q_SUPER = 512
ROW_CHUNK = 32
SCRATCH_SETS = 2
BIAS_TERMS = 3
TILES_PER_STEP = 2
ROW_TILE = 256
PAGES_PER_STEP = 16
VMEM_LIMIT_BYTES = 56 * 1024 * 1024

F32 = jnp.float32
BF16 = jnp.bfloat16
SIGN_BIT = np.int32(-2 ** 31)
LOG2E = math.log2(math.e)
Q_SCALE = HEAD_DIM ** -0.5 * LOG2E


def _dot(a, b):
    return jnp.dot(a, b, preferred_element_type=F32)


def _dot_nt(a, b):
    return lax.dot_general(a, b, (((1,), (1,)), ((), ())), preferred_element_type=F32)


def _rms(x, g):
    return x * lax.rsqrt(jnp.mean(x * x, axis=-1, keepdims=True) + EPS) * g


def _sigmoid(x):
    return 1.0 / (1.0 + jnp.exp(-x))


def _silu(x):
    return x * _sigmoid(x)


def _const_spec(shape):
    zeros = (0,) * len(shape)
    return pl.BlockSpec(shape, lambda *_: zeros, pipeline_mode=pl.Buffered(1))


class _Layer(NamedTuple):
    stack: jax.Array
    index: int


def _weight(w):
    if isinstance(w, _Layer):
        shape = w.stack.shape[1:]
        idx = (w.index,) + (0,) * len(shape)
        return w.stack, pl.BlockSpec((None,) + shape, lambda *_: idx, pipeline_mode=pl.Buffered(1))
    return w, _const_spec(w.shape)


def _ple(h1, p, wple_ref, gple_ref, wgate_ref):
    gate = _sigmoid(_dot(_rms(h1, gple_ref[...]).astype(BF16), wgate_ref[...]))
    return h1 + _dot(p.astype(BF16), wple_ref[...]) * gate


def _a_layer_kernel(*refs, chunk, with_vn, with_kv, kv_transposed):
    (h_ref, p_ref, gn_ref, win_ref, lng_ref, lnb_ref, ws_ref, bs_ref, wout_ref,
     wple_ref, gple_ref, wgate_ref) = refs[:12]
    pos = 12
    if with_kv:
        gkv_ref, wkv_ref = refs[pos:pos + 2]
        pos += 2
    h_out = refs[pos]
    pos += 1
    if with_vn:
        vn_out = refs[pos]
        pos += 1
    if with_kv:
        k_out, v_out, kb_out, vb_out = refs[pos:pos + 4]
        pos += 4
    vn_scr, m_scr = refs[pos:pos + 2]

    x = h_ref[...]
    xn = _rms(x, gn_ref[...]).astype(BF16)

    v = _dot(xn, win_ref[:, E_A:2 * E_A])
    mu = jnp.mean(v, axis=-1, keepdims=True)
    vc = v - mu
    vn = vc * lax.rsqrt(jnp.mean(vc * vc, axis=-1, keepdims=True) + EPS) * lng_ref[...] + lnb_ref[...]
    if with_vn:
        vn_out[...] = vn
    vn_scr[...] = vn.astype(BF16)

    row = lax.broadcasted_iota(jnp.int32, (CHUNK, CHUNK), 0)
    col = lax.broadcasted_iota(jnp.int32, (CHUNK, CHUNK), 1)
    shift = chunk.bit_length() - 1
    mix_mask = ((row >> shift) == (col >> shift)) & (col <= row)

    rows = x.shape[0]
    for g in range(N_GROUPS_A):
        lo, hi = g * GROUP_A, (g + 1) * GROUP_A
        wsm = jnp.where(mix_mask, ws_ref[g], 0.0).astype(BF16)
        bias = bs_ref[g]
        bias = jnp.concatenate([bias] * (GROUP_A // LANES), axis=1)
        u = _dot(xn, win_ref[:, lo:hi])
        gt = _dot(xn, win_ref[:, 2 * E_A + lo:2 * E_A + hi])
        ug = u * _silu(gt)
        for c in range(rows // CHUNK):
            r0, r1 = c * CHUNK, (c + 1) * CHUNK
            s = _dot(wsm, vn_scr[r0:r1, lo:hi]) + bias
            m_scr[r0:r1, lo:hi] = (ug[r0:r1] * s).astype(BF16)

    h1 = x + _dot(m_scr[...], wout_ref[...])
    h2 = _ple(h1, p_ref[...], wple_ref, gple_ref, wgate_ref)
    h_out[...] = h2
    if with_kv:
        kv = _dot(_rms(h2, gkv_ref[...]).astype(BF16), wkv_ref[...])
        k = kv[:, :E_B]
        vv = kv[:, E_B:]
        if kv_transposed:
            k_out[0] = k.T
            v_out[0] = vv.T
        else:
            k_out[...] = k
            v_out[...] = vv
        kb_out[...] = k.astype(BF16)
        vb_out[...] = vv.astype(BF16)


def _a_layer(h, p, gn, win, lng, lnb, ws, bs, wout, wple, gple, wgate, gkv=None, wkv=None, *,
             chunk, with_vn, kv_seq=None):
    rows = h.shape[0]
    with_kv = gkv is not None
    tm = ROW_TILE
    row_spec = lambda width: pl.BlockSpec((tm, width), lambda i: (i, 0))
    args = [h, p, gn, win, lng, lnb, ws, bs, wout, wple, gple, wgate]
    if with_kv:
        args += [gkv, wkv]
    weights, weight_specs = zip(*[_weight(a) for a in args[2:]])
    args = args[:2] + list(weights)
    in_specs = [row_spec(D_MODEL), row_spec(PLE_DIM)] + list(weight_specs)
    out_shape = [jax.ShapeDtypeStruct((rows, D_MODEL), F32)]
    out_specs = [row_spec(D_MODEL)]
    if with_vn:
        out_shape.append(jax.ShapeDtypeStruct((rows, E_A), F32))
        out_specs.append(row_spec(E_A))
    if with_kv:
        if kv_seq is None:
            out_shape += [jax.ShapeDtypeStruct((rows, E_B), F32)] * 2
            out_specs += [row_spec(E_B)] * 2
        else:
            per_seq = kv_seq // tm
            out_shape += [jax.ShapeDtypeStruct((rows // kv_seq, E_B, kv_seq), F32)] * 2
            out_specs += [pl.BlockSpec((1, E_B, tm), lambda i: (i // per_seq, 0, i % per_seq))] * 2
        out_shape += [jax.ShapeDtypeStruct((rows, E_B), BF16)] * 2
        out_specs += [row_spec(E_B)] * 2
    return pl.pallas_call(
        functools.partial(_a_layer_kernel, chunk=chunk, with_vn=with_vn, with_kv=with_kv,
                          kv_transposed=kv_seq is not None),
        grid=(rows // tm,),
        in_specs=in_specs,
        out_specs=out_specs,
        out_shape=out_shape,
        scratch_shapes=[pltpu.VMEM((tm, E_A), BF16), pltpu.VMEM((tm, E_A), BF16)],
        compiler_params=pltpu.CompilerParams(dimension_semantics=("arbitrary",),
                                             vmem_limit_bytes=VMEM_LIMIT_BYTES),
        name="a_layer",
    )(*args)


def _b_pre_kernel(h_ref, gn_ref, win_ref, q_out, sg_out):
    xn = _rms(h_ref[...], gn_ref[...]).astype(BF16)
    q = _dot(xn, win_ref[:, :E_B]) * Q_SCALE
    q_out[...] = q.astype(q_out.dtype)
    sg_out[...] = _silu(_dot(xn, win_ref[:, E_B:]))


def _b_pre(h, gn, win, q_dtype):
    rows = h.shape[0]
    tm = ROW_TILE
    row_spec = pl.BlockSpec((tm, D_MODEL), lambda i: (i, 0))
    win_op, win_spec = _weight(win)
    return pl.pallas_call(
        _b_pre_kernel,
        grid=(rows // tm,),
        in_specs=[row_spec, _const_spec(gn.shape), win_spec],
        out_specs=[row_spec, row_spec],
        out_shape=[jax.ShapeDtypeStruct((rows, E_B), q_dtype), jax.ShapeDtypeStruct((rows, E_B), F32)],
        compiler_params=pltpu.CompilerParams(dimension_semantics=("arbitrary",),
                                             vmem_limit_bytes=VMEM_LIMIT_BYTES),
        name="b_pre",
    )(h, gn, win_op)


def _b_post_kernel(h_ref, m_ref, p_ref, wout_ref, wple_ref, gple_ref, wgate_ref, gfin_ref, h_out, *,
                   final_norm):
    h1 = h_ref[...] + _dot(m_ref[...].astype(BF16), wout_ref[...])
    h2 = _ple(h1, p_ref[...], wple_ref, gple_ref, wgate_ref)
    if final_norm:
        h2 = _rms(h2, gfin_ref[...])
    h_out[...] = h2


def _b_post(h, m, p, wout, wple, gple, wgate, gfin, *, final_norm):
    rows = h.shape[0]
    tm = ROW_TILE
    row_spec = lambda width: pl.BlockSpec((tm, width), lambda i: (i, 0))
    weights, weight_specs = zip(*[_weight(a) for a in (wout, wple, gple, wgate, gfin)])
    return pl.pallas_call(
        functools.partial(_b_post_kernel, final_norm=final_norm),
        grid=(rows // tm,),
        in_specs=[row_spec(D_MODEL), row_spec(E_B), row_spec(PLE_DIM)] + list(weight_specs),
        out_specs=row_spec(D_MODEL),
        out_shape=jax.ShapeDtypeStruct((rows, D_MODEL), F32),
        compiler_params=pltpu.CompilerParams(dimension_semantics=("arbitrary",),
                                             vmem_limit_bytes=VMEM_LIMIT_BYTES),
        name="b_post",
    )(h, m, p, *weights)


def _softplus2(z2):
    neg_abs = lax.bitcast_convert_type(lax.bitcast_convert_type(z2, jnp.int32) | SIGN_BIT, F32)
    return jnp.maximum(z2, 0.0) + jnp.log(1.0 + jnp.exp2(neg_abs)) * LOG2E


def _hi_lo(x):
    hi = x.astype(BF16)
    return hi, (x - hi.astype(F32)).astype(BF16)


def _sb_tile(z2, mask, cum_ref, carry):
    sp = _softplus2(z2)
    lsz = z2 - sp
    if mask is not None:
        sp = jnp.where(mask, sp, 0.0)
    cs = _dot(jnp.concatenate(_hi_lo(sp), axis=1), cum_ref[...])
    w = jnp.exp2(lsz - (cs[:, :KEY_BLOCK] + carry))
    if mask is not None:
        w = jnp.where(mask, w, 0.0)
    return w, cs[:, KEY_BLOCK:]


def _cum_matrix():
    j = jnp.arange(KEY_BLOCK)[:, None]
    s = jnp.arange(KEY_BLOCK)[None, :]
    strict = (j > s).astype(BF16)
    half = jnp.concatenate([strict, jnp.ones((KEY_BLOCK, LANES), BF16)], axis=1)
    return jnp.concatenate([half, half], axis=0)


def _attn_prompt_kernel(q_ref, k_ref, v_ref, sg_ref, kbias_ref, cum_ref, o_ref,
                        qs_scr, z_scr, lcat_scr, lsz_scr, cs_scr, w_scr, acc_scr, carry_scr):
    qi = pl.program_id(2)
    n_sub = Q_SUPER // Q_SUB
    m_rows = HEADS_PER_LANE_TILE * Q_SUPER
    units_per_super = Q_SUPER // KEY_UNIT
    halves = KEY_UNIT // KEY_BLOCK

    lane = lax.broadcasted_iota(jnp.int32, (Q_SUB, LANES), 1)
    first = lane < HEAD_DIM
    for tile in range(TILES_PER_STEP):
        for a in range(n_sub):
            q2 = q_ref[a * Q_SUB:(a + 1) * Q_SUB, tile * LANES:(tile + 1) * LANES]
            zero = jnp.zeros_like(q2)
            for e in range(HEADS_PER_LANE_TILE):
                rows = slice((2 * a + e) * Q_SUB, (2 * a + e + 1) * Q_SUB)
                qs_scr[tile, rows, 0:LANES] = jnp.where(first, q2, zero) if e == 0 else jnp.where(first, zero, q2)
                ones = (lane >= e * BIAS_TERMS) & (lane < (e + 1) * BIAS_TERMS)
                qs_scr[tile, rows, LANES:2 * LANES] = jnp.where(ones, 1.0, 0.0).astype(BF16)
    acc_scr[...] = jnp.zeros_like(acc_scr)
    carry_scr[...] = jnp.zeros_like(carry_scr)

    def unit(tile, ku, r0, diag, st):
        start = pl.multiple_of(ku * KEY_UNIT, KEY_UNIT)
        lanes = slice(tile * LANES, (tile + 1) * LANES)
        st = tile * SCRATCH_SETS + st
        keys = jnp.concatenate([k_ref[pl.ds(start, KEY_UNIT), lanes], kbias_ref[tile]], axis=1)
        z_scr[st, r0:, :] = _dot_nt(qs_scr[tile, r0:, :], keys)
        chunks = range(r0 // ROW_CHUNK, m_rows // ROW_CHUNK)

        def visible(c):
            a, t0 = (c * ROW_CHUNK) // (2 * Q_SUB), (c * ROW_CHUNK) % Q_SUB
            t = lax.broadcasted_iota(jnp.int32, (ROW_CHUNK, KEY_UNIT), 0) + (a * Q_SUB + t0)
            s = lax.broadcasted_iota(jnp.int32, (ROW_CHUNK, KEY_UNIT), 1) + diag * KEY_UNIT
            return s < t

        for c in chunks:
            rows = slice(c * ROW_CHUNK, (c + 1) * ROW_CHUNK)
            z2 = z_scr[st, rows, :]
            sp = _softplus2(z2)
            lsz_scr[st, rows, :] = z2 - sp
            if diag is not None:
                sp = jnp.where(visible(c), sp, 0.0)
            hi, lo = _hi_lo(sp)
            for h in range(halves):
                cols = slice(h * KEY_BLOCK, (h + 1) * KEY_BLOCK)
                lcat_scr[st, h, rows, 0:KEY_BLOCK] = hi[:, cols]
                lcat_scr[st, h, rows, KEY_BLOCK:2 * KEY_BLOCK] = lo[:, cols]
        for h in range(halves):
            cs_scr[st, h, r0:, :] = _dot(lcat_scr[st, h, r0:, :], cum_ref[...])
        for c in chunks:
            rows = slice(c * ROW_CHUNK, (c + 1) * ROW_CHUNK)
            carry = carry_scr[tile, rows, :]
            lsz = lsz_scr[st, rows, :]
            ws = [None] * halves
            for h in reversed(range(halves)):
                cs = cs_scr[st, h, rows, :]
                ws[h] = jnp.exp2(lsz[:, h * KEY_BLOCK:(h + 1) * KEY_BLOCK] - (cs[:, :KEY_BLOCK] + carry))
                carry = carry + cs[:, KEY_BLOCK:]
            carry_scr[tile, rows, :] = carry
            w = jnp.concatenate(ws, axis=1)
            if diag is not None:
                w = jnp.where(visible(c), w, 0.0)
            w_scr[st, rows, :] = w.astype(BF16)
        acc_scr[tile, r0:, :] += _dot(w_scr[st, r0:, :], v_ref[pl.ds(start, KEY_UNIT), lanes])

    for d in reversed(range(units_per_super)):
        for tile in range(TILES_PER_STEP):
            unit(tile, qi * units_per_super + d, d * (KEY_UNIT // Q_SUB) * 2 * Q_SUB, d, d % SCRATCH_SETS)

    def body(i, c):
        for u in range(SCRATCH_SETS):
            for tile in range(TILES_PER_STEP):
                unit(tile, (qi - i) * units_per_super - 1 - u, 0, None, u)
        return c

    lax.fori_loop(0, qi * (units_per_super // SCRATCH_SETS), body, 0)

    for tile in range(TILES_PER_STEP):
        lanes = slice(tile * LANES, (tile + 1) * LANES)
        for a in range(n_sub):
            rows = slice(a * Q_SUB, (a + 1) * Q_SUB)
            o2 = jnp.where(first, acc_scr[tile, (2 * a) * Q_SUB:(2 * a + 1) * Q_SUB, :],
                           acc_scr[tile, (2 * a + 1) * Q_SUB:(2 * a + 2) * Q_SUB, :])
            o_ref[rows, lanes] = (o2 * sg_ref[rows, lanes]).astype(o_ref.dtype)


def _bias_columns(bias2):
    terms, rest = [], bias2
    for _ in range(BIAS_TERMS):
        t = rest.astype(BF16)
        terms.append(t)
        rest = rest - t.astype(F32)
    cols = jnp.stack(terms, axis=1).reshape(E_B // LANES, HEADS_PER_LANE_TILE * BIAS_TERMS)
    cols = jnp.pad(cols, ((0, 0), (0, LANES - HEADS_PER_LANE_TILE * BIAS_TERMS)))
    return jnp.broadcast_to(cols[:, None, :], (E_B // LANES, KEY_UNIT, LANES))


def _attn_prompt(q, kb, vb, sg, bias2, cum, *, batch, seq):
    nq = seq // Q_SUPER
    n_tiles = E_B // LANES
    m_rows = HEADS_PER_LANE_TILE * Q_SUPER
    halves = KEY_UNIT // KEY_BLOCK
    width = TILES_PER_STEP * LANES
    sets = TILES_PER_STEP * SCRATCH_SETS
    q_spec = pl.BlockSpec((Q_SUPER, width), lambda b, hp, qi: (b * nq + qi, hp))
    kv_spec = pl.BlockSpec((seq, width), lambda b, hp, qi: (b, hp))
    kbias_spec = pl.BlockSpec((TILES_PER_STEP, KEY_UNIT, LANES), lambda b, hp, qi: (hp, 0, 0))
    return pl.pallas_call(
        _attn_prompt_kernel,
        grid=(batch, n_tiles // TILES_PER_STEP, nq),
        in_specs=[q_spec, kv_spec, kv_spec, q_spec, kbias_spec, _const_spec(cum.shape)],
        out_specs=q_spec,
        out_shape=jax.ShapeDtypeStruct((batch * seq, E_B), BF16),
        scratch_shapes=[pltpu.VMEM((TILES_PER_STEP, m_rows, 2 * LANES), BF16),
                        pltpu.VMEM((sets, m_rows, KEY_UNIT), F32),
                        pltpu.VMEM((sets, halves, m_rows, 2 * KEY_BLOCK), BF16),
                        pltpu.VMEM((sets, m_rows, KEY_UNIT), F32),
                        pltpu.VMEM((sets, halves, m_rows, KEY_BLOCK + LANES), F32),
                        pltpu.VMEM((sets, m_rows, KEY_UNIT), BF16),
                        pltpu.VMEM((TILES_PER_STEP, m_rows, LANES), F32),
                        pltpu.VMEM((TILES_PER_STEP, m_rows, LANES), F32)],
        compiler_params=pltpu.CompilerParams(dimension_semantics=("arbitrary",) * 3,
                                             vmem_limit_bytes=VMEM_LIMIT_BYTES),
        name="attn_prompt",
    )(q, kb, vb, sg, _bias_columns(bias2), cum)


def _attn_sample_kernel(*refs, dec_seq, n_steps):
    pt_ref, q_ref, sg_ref, bias_ref, kn_ref, vn_ref, cum_ref = refs[:7]
    k_refs = refs[7:7 + PAGES_PER_STEP]
    v_refs = refs[7 + PAGES_PER_STEP:7 + 2 * PAGES_PER_STEP]
    o_ref, q_scr, acc_scr, carry_scr, pad_scr = refs[7 + 2 * PAGES_PER_STEP:]
    del pt_ref
    step = pl.program_id(1)
    n_rows = dec_seq * N_HEADS

    head = lax.broadcasted_iota(jnp.int32, (N_HEADS, E_B), 0)
    feat = lax.broadcasted_iota(jnp.int32, (N_HEADS, E_B), 1)
    head_mask = (feat >> (HEAD_DIM.bit_length() - 1)) == head

    @pl.when(step == 0)
    def _():
        q = q_ref[0]
        for t in range(dec_seq):
            rep = jnp.broadcast_to(q[t:t + 1, :], (N_HEADS, E_B))
            q_scr[t * N_HEADS:(t + 1) * N_HEADS, :] = jnp.where(head_mask, rep, 0.0).astype(BF16)
        row = lax.broadcasted_iota(jnp.int32, (n_rows, KEY_BLOCK), 0)
        key = lax.broadcasted_iota(jnp.int32, (n_rows, KEY_BLOCK), 1)
        new_mask = key < (row >> (N_HEADS.bit_length() - 1))
        pad_scr[...] = jnp.zeros_like(pad_scr)
        pad_scr[0:dec_seq, :] = kn_ref[0]
        z2 = _dot_nt(q_scr[...], pad_scr[...].astype(BF16)) + bias_ref[...]
        w, inc = _sb_tile(z2, new_mask, cum_ref, jnp.zeros((n_rows, KEY_BLOCK), F32))
        pad_scr[0:dec_seq, :] = vn_ref[0]
        acc_scr[...] = _dot(w.astype(BF16), pad_scr[...].astype(BF16))
        carry_scr[...] = inc

    q_rep = q_scr[...]
    lszs, parts = [], []
    for u in range(PAGES_PER_STEP):
        z2 = _dot(q_rep, k_refs[u][0].astype(BF16)) + bias_ref[...]
        sp = _softplus2(z2)
        lszs.append(z2 - sp)
        parts.append(jnp.concatenate(_hi_lo(sp), axis=1))
    cs = _dot(jnp.concatenate(parts, axis=0), cum_ref[...])
    carry = carry_scr[...]
    acc = acc_scr[...]
    for u in range(PAGES_PER_STEP):
        cs_u = cs[u * n_rows:(u + 1) * n_rows]
        w = jnp.exp2(lszs[u] - (cs_u[:, :KEY_BLOCK] + carry))
        acc = acc + _dot_nt(w.astype(BF16), v_refs[u][0].astype(BF16))
        carry = carry + cs_u[:, KEY_BLOCK:]
    carry_scr[...] = carry
    acc_scr[...] = acc

    @pl.when(step == n_steps - 1)
    def _():
        sg = sg_ref[0]
        for t in range(dec_seq):
            blk = jnp.where(head_mask, acc_scr[t * N_HEADS:(t + 1) * N_HEADS, :], 0.0)
            o_ref[0, t:t + 1, :] = jnp.sum(blk, axis=0, keepdims=True) * sg[t:t + 1, :]


def _attn_sample(q, sg, bias_rows, k_new, v_new, cache_k, cache_v, page_table, cum):
    n_seq, dec_seq, _ = q.shape
    n_pages = page_table.shape[1]
    n_steps = n_pages // PAGES_PER_STEP
    n_rows = dec_seq * N_HEADS
    seq_spec = pl.BlockSpec((1, dec_seq, E_B), lambda b, j, pt: (b, 0, 0))

    def page_spec(u):
        return pl.BlockSpec((1, E_B, PAGE_SIZE),
                            lambda b, j, pt: (pt[b, n_pages - 1 - (j * PAGES_PER_STEP + u)], 0, 0))

    const = lambda a: pl.BlockSpec(a.shape, lambda b, j, pt: (0,) * a.ndim, pipeline_mode=pl.Buffered(1))
    page_specs = [page_spec(u) for u in range(PAGES_PER_STEP)]
    grid_spec = pltpu.PrefetchScalarGridSpec(
        num_scalar_prefetch=1,
        grid=(n_seq, n_steps),
        in_specs=[seq_spec, seq_spec, const(bias_rows), seq_spec, seq_spec, const(cum)] + page_specs + page_specs,
        out_specs=seq_spec,
        scratch_shapes=[pltpu.VMEM((n_rows, E_B), BF16), pltpu.VMEM((n_rows, E_B), F32),
                        pltpu.VMEM((n_rows, KEY_BLOCK), F32), pltpu.VMEM((KEY_BLOCK, E_B), F32)],
    )
    return pl.pallas_call(
        functools.partial(_attn_sample_kernel, dec_seq=dec_seq, n_steps=n_steps),
        grid_spec=grid_spec,
        out_shape=jax.ShapeDtypeStruct((n_seq, dec_seq, E_B), F32),
        compiler_params=pltpu.CompilerParams(dimension_semantics=("arbitrary", "arbitrary"),
                                             vmem_limit_bytes=VMEM_LIMIT_BYTES),
        name="attn_sample",
    )(page_table, q, sg, bias_rows, k_new, v_new, cum, *([cache_k] * PAGES_PER_STEP),
      *([cache_v] * PAGES_PER_STEP))


def _row(a):
    return a.reshape(1, -1)


def _trunk(x, p, wts, attend, *, chunk, keep_chunk_state, kv_seq=None):
    (g_norm, w_in_a, ln_v_g, ln_v_b, ws_tiles, bs_tiles, w_out_a, g_kv, w_kv, w_in_b, w_out_b,
     w_ple, g_ple, w_ple_gate, g_final) = wts
    h = x
    chunk_rows = []
    k = v = kb = vb = None
    for i in range(N_A_LAYERS):
        last = i == N_A_LAYERS - 1
        outs = _a_layer(h, p[i], _row(g_norm[i]), _Layer(w_in_a, i), _row(ln_v_g[i]), _row(ln_v_b[i]),
                        _Layer(ws_tiles, i), _Layer(bs_tiles, i), _Layer(w_out_a, i), _Layer(w_ple, i),
                        _row(g_ple[i]), _Layer(w_ple_gate, i),
                        _row(g_kv) if last else None, w_kv if last else None,
                        chunk=chunk, with_vn=keep_chunk_state, kv_seq=kv_seq if last else None)
        h = outs[0]
        pos = 1
        if keep_chunk_state:
            chunk_rows.append(outs[pos])
            pos += 1
        if last:
            k, v, kb, vb = outs[pos:pos + 4]
    for j in range(DEPTH - N_A_LAYERS):
        i = N_A_LAYERS + j
        q, sg = attend.pre(h, _row(g_norm[i]), _Layer(w_in_b, j))
        m = attend(j, q, sg, k, v, kb, vb)
        h = _b_post(h, m, p[i], _Layer(w_out_b, j), _Layer(w_ple, i), _row(g_ple[i]), _Layer(w_ple_gate, i),
                    _row(g_final),
                    final_norm=(i == DEPTH - 1))
    return h, k, v, chunk_rows


class _PromptAttend:
    def __init__(self, batch, seq, b_sb, cum):
        self.batch, self.seq, self.b_sb, self.cum = batch, seq, b_sb, cum

    def pre(self, h, gn, win):
        return _b_pre(h, gn, win, BF16)

    def __call__(self, j, q, sg, k, v, kb, vb):
        return _attn_prompt(q, kb, vb, sg, self.b_sb[j] * LOG2E, self.cum, batch=self.batch, seq=self.seq)


class _SampleAttend:
    def __init__(self, n_seq, dec_seq, b_sb, cum, cache_k, cache_v, page_table):
        self.n_seq, self.dec_seq, self.b_sb, self.cum = n_seq, dec_seq, b_sb, cum
        self.cache_k, self.cache_v, self.page_table = cache_k, cache_v, page_table

    def pre(self, h, gn, win):
        return _b_pre(h, gn, win, F32)

    def __call__(self, j, q, sg, k, v, kb, vb):
        shape = (self.n_seq, self.dec_seq, E_B)
        bias_rows = jnp.broadcast_to(jnp.tile(self.b_sb[j] * LOG2E, self.dec_seq)[:, None],
                                     (self.dec_seq * N_HEADS, KEY_BLOCK))
        m = _attn_sample(q.reshape(shape), sg.reshape(shape), bias_rows, k.reshape(shape), v.reshape(shape),
                         self.cache_k, self.cache_v, self.page_table, self.cum)
        return m.reshape(self.n_seq * self.dec_seq, E_B)


def kernel(x_prompt, x_sample, cache_k, cache_v, page_table, p_prompt, p_sample, g_norm, w_in_a, ln_v_g, ln_v_b,
           w_spatial, b_spatial, w_out_a, g_kv, w_kv, w_in_b, w_out_b, b_sb, w_ple, g_ple, w_ple_gate, g_final):
    batch, seq, _ = x_prompt.shape
    n_seq, dec_seq, _ = x_sample.shape
    n_pool = cache_k.shape[0]
    assert seq % ROW_TILE == 0 and (n_seq * dec_seq) % ROW_TILE == 0 and seq % Q_SUPER == 0
    assert (Q_SUPER // KEY_UNIT) % SCRATCH_SETS == 0 and page_table.shape[1] % PAGES_PER_STEP == 0
    assert CHUNK % dec_seq == 0 and dec_seq & (dec_seq - 1) == 0

    bf = lambda a: a.astype(BF16)
    cum = _cum_matrix()
    pages_t = lambda c: jnp.transpose(c, (0, 2, 3, 1)).reshape(n_pool, E_B, PAGE_SIZE)

    def weights(chunk):
        rep = CHUNK // chunk
        ws_tiles = jnp.tile(w_spatial[:, :, :chunk, :chunk], (1, 1, rep, rep))
        bs_tiles = jnp.broadcast_to(jnp.tile(b_spatial[:, :, :chunk], (1, 1, rep))[..., None],
                                    (N_A_LAYERS, N_GROUPS_A, CHUNK, LANES))
        return (g_norm, bf(w_in_a), ln_v_g, ln_v_b, ws_tiles, bs_tiles, bf(w_out_a), g_kv, bf(w_kv), bf(w_in_b),
                bf(w_out_b), bf(w_ple), g_ple, bf(w_ple_gate), g_final)

    y_p, k_p, v_p, _ = _trunk(
        x_prompt.reshape(batch * seq, D_MODEL), p_prompt.reshape(DEPTH, batch * seq, PLE_DIM), weights(CHUNK),
        _PromptAttend(batch, seq, b_sb, cum), chunk=CHUNK, keep_chunk_state=False, kv_seq=seq)
    rows_last = lambda a: jnp.transpose(a.reshape(batch, N_HEADS, HEAD_DIM, seq), (0, 3, 1, 2))

    y_s, k_s, v_s, chunk_rows = _trunk(
        x_sample.reshape(n_seq * dec_seq, D_MODEL), p_sample.reshape(DEPTH, n_seq * dec_seq, PLE_DIM),
        weights(dec_seq),
        _SampleAttend(n_seq, dec_seq, b_sb, cum, pages_t(cache_k), pages_t(cache_v), page_table),
        chunk=dec_seq, keep_chunk_state=True)

    return (y_p.reshape(batch, seq, D_MODEL),
            y_s.reshape(n_seq, dec_seq, D_MODEL),
            rows_last(k_p),
            rows_last(v_p),
            k_s.reshape(n_seq, dec_seq, N_HEADS, HEAD_DIM),
            v_s.reshape(n_seq, dec_seq, N_HEADS, HEAD_DIM),
            jnp.stack(chunk_rows).reshape(N_A_LAYERS, n_seq, dec_seq, E_A))
```

```python
import functools
import math
from typing import NamedTuple

import jax
import jax.numpy as jnp
import numpy as np
from jax import lax
from jax.experimental import pallas as pl
from jax.experimental.pallas import tpu as pltpu

D_MODEL = 1024
DEPTH = 4
N_A_LAYERS = DEPTH // 2
CHUNK = 128
E_A = 2 * D_MODEL
N_GROUPS_A = 8
GROUP_A = E_A // N_GROUPS_A
N_HEADS = 16
HEAD_DIM = 64
E_B = N_HEADS * HEAD_DIM
PLE_DIM = 256
PAGE_SIZE = 128
EPS = 1e-6

LANES = 128
HEADS_PER_LANE_TILE = LANES // HEAD_DIM
KEY_BLOCK = 128
KEY_UNIT = 2 * KEY_BLOCK
Q_SUB = 128
Q_SUPER = 512
ROW_CHUNK = 32
SCRATCH_SETS = 2
BIAS_TERMS = 3
TILES_PER_STEP = 2
RIDE_STEPS = 2
ROW_TILE = 256
PAGES_PER_STEP = 16
VMEM_LIMIT_BYTES = 56 * 1024 * 1024

F32 = jnp.float32
BF16 = jnp.bfloat16
SIGN_BIT = np.int32(-2 ** 31)
LOG2E = math.log2(math.e)
Q_SCALE = HEAD_DIM ** -0.5 * LOG2E


def _dot(a, b):
    return jnp.dot(a, b, preferred_element_type=F32)


def _dot_nt(a, b):
    return lax.dot_general(a, b, (((1,), (1,)), ((), ())), preferred_element_type=F32)


def _rms(x, g):
    return x * lax.rsqrt(jnp.mean(x * x, axis=-1, keepdims=True) + EPS) * g


def _sigmoid(x):
    return 1.0 / (1.0 + jnp.exp(-x))


def _silu(x):
    return x * _sigmoid(x)


def _const_spec(shape):
    zeros = (0,) * len(shape)
    return pl.BlockSpec(shape, lambda *_: zeros, pipeline_mode=pl.Buffered(1))


class _Layer(NamedTuple):
    stack: jax.Array
    index: int


def _row_tiled(a, tm, tile_of=lambda i: i):
    if isinstance(a, _Layer):
        layer, width = a.index, a.stack.shape[2]
        return a.stack, pl.BlockSpec((None, tm, width), lambda *g: (layer, tile_of(*g), 0))
    return a, pl.BlockSpec((tm, a.shape[1]), lambda *g: (tile_of(*g), 0))


def _weight(w):
    if isinstance(w, _Layer):
        shape = w.stack.shape[1:]
        idx = (w.index,) + (0,) * len(shape)
        return w.stack, pl.BlockSpec((None,) + shape, lambda *_: idx, pipeline_mode=pl.Buffered(1))
    return w, _const_spec(w.shape)


def _ple(h1, p, wple_ref, gple_ref, wgate_ref):
    gate = _sigmoid(_dot(_rms(h1, gple_ref[...]).astype(BF16), wgate_ref[...]))
    return h1 + _dot(p.astype(BF16), wple_ref[...]) * gate


def _a_layer_kernel(*refs, chunk, with_vn, with_kv, kv_transposed):
    (h_ref, p_ref, gn_ref, win_ref, lng_ref, lnb_ref, ws_ref, bs_ref, wout_ref,
     wple_ref, gple_ref, wgate_ref) = refs[:12]
    pos = 12
    if with_kv:
        gkv_ref, wkv_ref = refs[pos:pos + 2]
        pos += 2
    h_out = refs[pos]
    pos += 1
    if with_vn:
        vn_out = refs[pos]
        pos += 1
    if with_kv:
        k_out, v_out, kb_out, vb_out = refs[pos:pos + 4]
        pos += 4
    vn_scr, m_scr = refs[pos:pos + 2]

    x = h_ref[...]
    xn = _rms(x, gn_ref[...]).astype(BF16)

    v = _dot(xn, win_ref[:, E_A:2 * E_A])
    mu = jnp.mean(v, axis=-1, keepdims=True)
    vc = v - mu
    vn = vc * lax.rsqrt(jnp.mean(vc * vc, axis=-1, keepdims=True) + EPS) * lng_ref[...] + lnb_ref[...]
    if with_vn:
        vn_out[...] = vn
    vn_scr[...] = vn.astype(BF16)

    row = lax.broadcasted_iota(jnp.int32, (CHUNK, CHUNK), 0)
    col = lax.broadcasted_iota(jnp.int32, (CHUNK, CHUNK), 1)
    shift = chunk.bit_length() - 1
    mix_mask = ((row >> shift) == (col >> shift)) & (col <= row)

    rows = x.shape[0]
    for g in range(N_GROUPS_A):
        lo, hi = g * GROUP_A, (g + 1) * GROUP_A
        wsm = jnp.where(mix_mask, ws_ref[g], 0.0).astype(BF16)
        bias = bs_ref[g]
        bias = jnp.concatenate([bias] * (GROUP_A // LANES), axis=1)
        u = _dot(xn, win_ref[:, lo:hi])
        gt = _dot(xn, win_ref[:, 2 * E_A + lo:2 * E_A + hi])
        ug = u * _silu(gt)
        for c in range(rows // CHUNK):
            r0, r1 = c * CHUNK, (c + 1) * CHUNK
            s = _dot(wsm, vn_scr[r0:r1, lo:hi]) + bias
            m_scr[r0:r1, lo:hi] = (ug[r0:r1] * s).astype(BF16)

    h1 = x + _dot(m_scr[...], wout_ref[...])
    h2 = _ple(h1, p_ref[...], wple_ref, gple_ref, wgate_ref)
    h_out[...] = h2
    if with_kv:
        kv = _dot(_rms(h2, gkv_ref[...]).astype(BF16), wkv_ref[...])
        k = kv[:, :E_B]
        vv = kv[:, E_B:]
        if kv_transposed:
            k_out[0] = k.T
            v_out[0] = vv.T
        else:
            k_out[...] = k
            v_out[...] = vv
        kb_out[...] = k.astype(BF16)
        vb_out[...] = vv.astype(BF16)


def _a_layer(h, p, gn, win, lng, lnb, ws, bs, wout, wple, gple, wgate, gkv=None, wkv=None, *,
             chunk, with_vn, kv_seq=None):
    rows = h.shape[0]
    with_kv = gkv is not None
    tm = ROW_TILE
    row_spec = lambda width: pl.BlockSpec((tm, width), lambda i: (i, 0))
    args = [h, p, gn, win, lng, lnb, ws, bs, wout, wple, gple, wgate]
    if with_kv:
        args += [gkv, wkv]
    weights, weight_specs = zip(*[_weight(a) for a in args[2:]])
    p_op, p_spec = _row_tiled(p, tm)
    args = [h, p_op] + list(weights)
    in_specs = [row_spec(D_MODEL), p_spec] + list(weight_specs)
    out_shape = [jax.ShapeDtypeStruct((rows, D_MODEL), F32)]
    out_specs = [row_spec(D_MODEL)]
    if with_vn:
        out_shape.append(jax.ShapeDtypeStruct((rows, E_A), F32))
        out_specs.append(row_spec(E_A))
    if with_kv:
        if kv_seq is None:
            out_shape += [jax.ShapeDtypeStruct((rows, E_B), F32)] * 2
            out_specs += [row_spec(E_B)] * 2
        else:
            per_seq = kv_seq // tm
            out_shape += [jax.ShapeDtypeStruct((rows // kv_seq, E_B, kv_seq), F32)] * 2
            out_specs += [pl.BlockSpec((1, E_B, tm), lambda i: (i // per_seq, 0, i % per_seq))] * 2
        out_shape += [jax.ShapeDtypeStruct((rows, E_B), BF16)] * 2
        out_specs += [row_spec(E_B)] * 2
    return pl.pallas_call(
        functools.partial(_a_layer_kernel, chunk=chunk, with_vn=with_vn, with_kv=with_kv,
                          kv_transposed=kv_seq is not None),
        grid=(rows // tm,),
        in_specs=in_specs,
        out_specs=out_specs,
        out_shape=out_shape,
        scratch_shapes=[pltpu.VMEM((tm, E_A), BF16), pltpu.VMEM((tm, E_A), BF16)],
        compiler_params=pltpu.CompilerParams(dimension_semantics=("arbitrary",),
                                             vmem_limit_bytes=VMEM_LIMIT_BYTES),
        name="a_layer",
    )(*args)


def _b_pre_rows(h, gn_ref, win_ref):
    xn = _rms(h, gn_ref[...]).astype(BF16)
    return _dot(xn, win_ref[:, :E_B]) * Q_SCALE, _silu(_dot(xn, win_ref[:, E_B:]))


def _b_pre_kernel(h_ref, gn_ref, win_ref, q_out, sg_out):
    q, sg = _b_pre_rows(h_ref[...], gn_ref, win_ref)
    q_out[...] = q.astype(q_out.dtype)
    sg_out[...] = sg


def _b_pre(h, gn, win, q_dtype):
    rows = h.shape[0]
    tm = ROW_TILE
    row_spec = pl.BlockSpec((tm, D_MODEL), lambda i: (i, 0))
    win_op, win_spec = _weight(win)
    return pl.pallas_call(
        _b_pre_kernel,
        grid=(rows // tm,),
        in_specs=[row_spec, _const_spec(gn.shape), win_spec],
        out_specs=[row_spec, row_spec],
        out_shape=[jax.ShapeDtypeStruct((rows, E_B), q_dtype), jax.ShapeDtypeStruct((rows, E_B), F32)],
        compiler_params=pltpu.CompilerParams(dimension_semantics=("arbitrary",),
                                             vmem_limit_bytes=VMEM_LIMIT_BYTES),
        name="b_pre",
    )(h, gn, win_op)


def _b_post_rows(h, m, p, wout_ref, wple_ref, gple_ref, wgate_ref):
    h1 = h + _dot(m.astype(BF16), wout_ref[...])
    return _ple(h1, p, wple_ref, gple_ref, wgate_ref)


def _b_post_kernel(h_ref, m_ref, p_ref, wout_ref, wple_ref, gple_ref, wgate_ref, gfin_ref, h_out, *,
                   final_norm):
    h2 = _b_post_rows(h_ref[...], m_ref[...], p_ref[...], wout_ref, wple_ref, gple_ref, wgate_ref)
    if final_norm:
        h2 = _rms(h2, gfin_ref[...])
    h_out[...] = h2


def _b_post(h, m, p, wout, wple, gple, wgate, gfin, *, final_norm):
    rows = h.shape[0]
    tm = ROW_TILE
    row_spec = lambda width: pl.BlockSpec((tm, width), lambda i: (i, 0))
    weights, weight_specs = zip(*[_weight(a) for a in (wout, wple, gple, wgate, gfin)])
    p_op, p_spec = _row_tiled(p, tm)
    return pl.pallas_call(
        functools.partial(_b_post_kernel, final_norm=final_norm),
        grid=(rows // tm,),
        in_specs=[row_spec(D_MODEL), row_spec(E_B), p_spec] + list(weight_specs),
        out_specs=row_spec(D_MODEL),
        out_shape=jax.ShapeDtypeStruct((rows, D_MODEL), F32),
        compiler_params=pltpu.CompilerParams(dimension_semantics=("arbitrary",),
                                             vmem_limit_bytes=VMEM_LIMIT_BYTES),
        name="b_post",
    )(h, m, p_op, *weights)


def _softplus2(z2):
    neg_abs = lax.bitcast_convert_type(lax.bitcast_convert_type(z2, jnp.int32) | SIGN_BIT, F32)
    return jnp.maximum(z2, 0.0) + jnp.log(1.0 + jnp.exp2(neg_abs)) * LOG2E


def _hi_lo(x):
    hi = x.astype(BF16)
    return hi, (x - hi.astype(F32)).astype(BF16)


def _sb_tile(z2, mask, cum_ref, carry):
    sp = _softplus2(z2)
    lsz = z2 - sp
    if mask is not None:
        sp = jnp.where(mask, sp, 0.0)
    cs = _dot(jnp.concatenate(_hi_lo(sp), axis=1), cum_ref[...])
    w = jnp.exp2(lsz - (cs[:, :KEY_BLOCK] + carry))
    if mask is not None:
        w = jnp.where(mask, w, 0.0)
    return w, cs[:, KEY_BLOCK:]


def _cum_matrix():
    j = jnp.arange(KEY_BLOCK)[:, None]
    s = jnp.arange(KEY_BLOCK)[None, :]
    strict = (j > s).astype(BF16)
    half = jnp.concatenate([strict, jnp.ones((KEY_BLOCK, LANES), BF16)], axis=1)
    return jnp.concatenate([half, half], axis=0)


def _attn_prompt_kernel(q_ref, k_ref, v_ref, sg_ref, kbias_ref, cum_ref, o_ref,
                        qs_scr, z_scr, lcat_scr, lsz_scr, cs_scr, w_scr, acc_scr, carry_scr):
    qi = pl.program_id(2)
    n_sub = Q_SUPER // Q_SUB
    m_rows = HEADS_PER_LANE_TILE * Q_SUPER
    units_per_super = Q_SUPER // KEY_UNIT
    halves = KEY_UNIT // KEY_BLOCK

    lane = lax.broadcasted_iota(jnp.int32, (Q_SUB, LANES), 1)
    first = lane < HEAD_DIM
    for tile in range(TILES_PER_STEP):
        for a in range(n_sub):
            q2 = q_ref[a * Q_SUB:(a + 1) * Q_SUB, tile * LANES:(tile + 1) * LANES]
            zero = jnp.zeros_like(q2)
            for e in range(HEADS_PER_LANE_TILE):
                rows = slice((2 * a + e) * Q_SUB, (2 * a + e + 1) * Q_SUB)
                qs_scr[tile, rows, 0:LANES] = jnp.where(first, q2, zero) if e == 0 else jnp.where(first, zero, q2)
                ones = (lane >= e * BIAS_TERMS) & (lane < (e + 1) * BIAS_TERMS)
                qs_scr[tile, rows, LANES:2 * LANES] = jnp.where(ones, 1.0, 0.0).astype(BF16)
    acc_scr[...] = jnp.zeros_like(acc_scr)
    carry_scr[...] = jnp.zeros_like(carry_scr)

    def unit(tile, ku, r0, diag, st):
        start = pl.multiple_of(ku * KEY_UNIT, KEY_UNIT)
        lanes = slice(tile * LANES, (tile + 1) * LANES)
        st = tile * SCRATCH_SETS + st
        keys = jnp.concatenate([k_ref[pl.ds(start, KEY_UNIT), lanes], kbias_ref[tile]], axis=1)
        z_scr[st, r0:, :] = _dot_nt(qs_scr[tile, r0:, :], keys)
        chunks = range(r0 // ROW_CHUNK, m_rows // ROW_CHUNK)

        def visible(c):
            a, t0 = (c * ROW_CHUNK) // (2 * Q_SUB), (c * ROW_CHUNK) % Q_SUB
            t = lax.broadcasted_iota(jnp.int32, (ROW_CHUNK, KEY_UNIT), 0) + (a * Q_SUB + t0)
            s = lax.broadcasted_iota(jnp.int32, (ROW_CHUNK, KEY_UNIT), 1) + diag * KEY_UNIT
            return s < t

        for c in chunks:
            rows = slice(c * ROW_CHUNK, (c + 1) * ROW_CHUNK)
            z2 = z_scr[st, rows, :]
            sp = _softplus2(z2)
            lsz_scr[st, rows, :] = z2 - sp
            if diag is not None:
                sp = jnp.where(visible(c), sp, 0.0)
            hi, lo = _hi_lo(sp)
            for h in range(halves):
                cols = slice(h * KEY_BLOCK, (h + 1) * KEY_BLOCK)
                lcat_scr[st, h, rows, 0:KEY_BLOCK] = hi[:, cols]
                lcat_scr[st, h, rows, KEY_BLOCK:2 * KEY_BLOCK] = lo[:, cols]
        for h in range(halves):
            cs_scr[st, h, r0:, :] = _dot(lcat_scr[st, h, r0:, :], cum_ref[...])
        for c in chunks:
            rows = slice(c * ROW_CHUNK, (c + 1) * ROW_CHUNK)
            carry = carry_scr[tile, rows, :]
            lsz = lsz_scr[st, rows, :]
            ws = [None] * halves
            for h in reversed(range(halves)):
                cs = cs_scr[st, h, rows, :]
                ws[h] = jnp.exp2(lsz[:, h * KEY_BLOCK:(h + 1) * KEY_BLOCK] - (cs[:, :KEY_BLOCK] + carry))
                carry = carry + cs[:, KEY_BLOCK:]
            carry_scr[tile, rows, :] = carry
            w = jnp.concatenate(ws, axis=1)
            if diag is not None:
                w = jnp.where(visible(c), w, 0.0)
            w_scr[st, rows, :] = w.astype(BF16)
        acc_scr[tile, r0:, :] += _dot(w_scr[st, r0:, :], v_ref[pl.ds(start, KEY_UNIT), lanes])

    for d in reversed(range(units_per_super)):
        for tile in range(TILES_PER_STEP):
            unit(tile, qi * units_per_super + d, d * (KEY_UNIT // Q_SUB) * 2 * Q_SUB, d, d % SCRATCH_SETS)

    def body(i, c):
        for u in range(SCRATCH_SETS):
            for tile in range(TILES_PER_STEP):
                unit(tile, (qi - i) * units_per_super - 1 - u, 0, None, u)
        return c

    lax.fori_loop(0, qi * (units_per_super // SCRATCH_SETS), body, 0)

    for tile in range(TILES_PER_STEP):
        lanes = slice(tile * LANES, (tile + 1) * LANES)
        for a in range(n_sub):
            rows = slice(a * Q_SUB, (a + 1) * Q_SUB)
            o2 = jnp.where(first, acc_scr[tile, (2 * a) * Q_SUB:(2 * a + 1) * Q_SUB, :],
                           acc_scr[tile, (2 * a + 1) * Q_SUB:(2 * a + 2) * Q_SUB, :])
            o_ref[rows, lanes] = (o2 * sg_ref[rows, lanes]).astype(o_ref.dtype)


def _bias_columns(bias2):
    terms, rest = [], bias2
    for _ in range(BIAS_TERMS):
        t = rest.astype(BF16)
        terms.append(t)
        rest = rest - t.astype(F32)
    cols = jnp.stack(terms, axis=1).reshape(E_B // LANES, HEADS_PER_LANE_TILE * BIAS_TERMS)
    cols = jnp.pad(cols, ((0, 0), (0, LANES - HEADS_PER_LANE_TILE * BIAS_TERMS)))
    return jnp.broadcast_to(cols[:, None, :], (E_B // LANES, KEY_UNIT, LANES))


def _attn_prompt(q, kb, vb, sg, bias2, cum, *, batch, seq):
    nq = seq // Q_SUPER
    n_tiles = E_B // LANES
    m_rows = HEADS_PER_LANE_TILE * Q_SUPER
    halves = KEY_UNIT // KEY_BLOCK
    width = TILES_PER_STEP * LANES
    sets = TILES_PER_STEP * SCRATCH_SETS
    q_spec = pl.BlockSpec((Q_SUPER, width), lambda b, hp, qi: (b * nq + qi, hp))
    kv_spec = pl.BlockSpec((seq, width), lambda b, hp, qi: (b, hp))
    kbias_spec = pl.BlockSpec((TILES_PER_STEP, KEY_UNIT, LANES), lambda b, hp, qi: (hp, 0, 0))
    return pl.pallas_call(
        _attn_prompt_kernel,
        grid=(batch, n_tiles // TILES_PER_STEP, nq),
        in_specs=[q_spec, kv_spec, kv_spec, q_spec, kbias_spec, _const_spec(cum.shape)],
        out_specs=q_spec,
        out_shape=jax.ShapeDtypeStruct((batch * seq, E_B), BF16),
        scratch_shapes=[pltpu.VMEM((TILES_PER_STEP, m_rows, 2 * LANES), BF16),
                        pltpu.VMEM((sets, m_rows, KEY_UNIT), F32),
                        pltpu.VMEM((sets, halves, m_rows, 2 * KEY_BLOCK), BF16),
                        pltpu.VMEM((sets, m_rows, KEY_UNIT), F32),
                        pltpu.VMEM((sets, halves, m_rows, KEY_BLOCK + LANES), F32),
                        pltpu.VMEM((sets, m_rows, KEY_UNIT), BF16),
                        pltpu.VMEM((TILES_PER_STEP, m_rows, LANES), F32),
                        pltpu.VMEM((TILES_PER_STEP, m_rows, LANES), F32)],
        compiler_params=pltpu.CompilerParams(dimension_semantics=("arbitrary",) * 3,
                                             vmem_limit_bytes=VMEM_LIMIT_BYTES),
        name="attn_prompt",
    )(q, kb, vb, sg, _bias_columns(bias2), cum)


_RIDER_ARITY = {None: (0, 0), "pre": (3, 2), "post_pre": (9, 3)}


def _ride(kind, ins, outs, h_scr, step):
    first = step % RIDE_STEPS == 0
    if kind == "pre":
        h_ref, gn_ref, win_ref = ins
        q_out, sg_out = outs

        @pl.when(first)
        def _():
            xn = _rms(h_ref[...], gn_ref[...]).astype(BF16)
            q_out[...] = (_dot(xn, win_ref[:, :E_B]) * Q_SCALE).astype(q_out.dtype)

        @pl.when(jnp.logical_not(first))
        def _():
            xn = _rms(h_ref[...], gn_ref[...]).astype(BF16)
            sg_out[...] = _silu(_dot(xn, win_ref[:, E_B:]))
    else:
        h_ref, m_ref, p_ref, wout_ref, wple_ref, gple_ref, wgate_ref, gn_ref, win_ref = ins
        h_out, q_out, sg_out = outs

        @pl.when(first)
        def _():
            h_scr[...] = _b_post_rows(h_ref[...], m_ref[...], p_ref[...], wout_ref, wple_ref, gple_ref, wgate_ref)
            h_out[...] = h_scr[...]

        @pl.when(jnp.logical_not(first))
        def _():
            q, sg = _b_pre_rows(h_scr[...], gn_ref, win_ref)
            q_out[...] = q.astype(q_out.dtype)
            sg_out[...] = sg


def _attn_sample_kernel(*refs, dec_seq, n_steps, rider):
    pt_ref, q_ref, sg_ref, bias_ref, kn_ref, vn_ref, cum_ref = refs[:7]
    k_refs = refs[7:7 + PAGES_PER_STEP]
    v_refs = refs[7 + PAGES_PER_STEP:7 + 2 * PAGES_PER_STEP]
    n_ride_in, n_ride_out = _RIDER_ARITY[rider]
    pos = 7 + 2 * PAGES_PER_STEP
    ride_in = refs[pos:pos + n_ride_in]
    o_ref = refs[pos + n_ride_in]
    ride_out = refs[pos + n_ride_in + 1:pos + n_ride_in + 1 + n_ride_out]
    q_scr, acc_scr, carry_scr, pad_scr = refs[pos + n_ride_in + 1 + n_ride_out:][:4]
    h_scr = refs[-1] if rider == "post_pre" else None
    del pt_ref
    step = pl.program_id(1)
    n_rows = dec_seq * N_HEADS
    when = (lambda cond: (lambda f: f())) if n_steps == 1 else pl.when

    if rider is not None:
        _ride(rider, ride_in, ride_out, h_scr, pl.program_id(0))

    head = lax.broadcasted_iota(jnp.int32, (N_HEADS, E_B), 0)
    feat = lax.broadcasted_iota(jnp.int32, (N_HEADS, E_B), 1)
    head_mask = (feat >> (HEAD_DIM.bit_length() - 1)) == head

    @when(step == 0)
    def _():
        q = q_ref[0]
        for t in range(dec_seq):
            rep = jnp.broadcast_to(q[t:t + 1, :], (N_HEADS, E_B))
            q_scr[t * N_HEADS:(t + 1) * N_HEADS, :] = jnp.where(head_mask, rep, 0.0).astype(BF16)
        row = lax.broadcasted_iota(jnp.int32, (n_rows, KEY_BLOCK), 0)
        key = lax.broadcasted_iota(jnp.int32, (n_rows, KEY_BLOCK), 1)
        new_mask = key < (row >> (N_HEADS.bit_length() - 1))
        pad_scr[...] = jnp.zeros_like(pad_scr)
        pad_scr[0:dec_seq, :] = kn_ref[0]
        z2 = _dot_nt(q_scr[...], pad_scr[...].astype(BF16)) + bias_ref[...]
        w, inc = _sb_tile(z2, new_mask, cum_ref, jnp.zeros((n_rows, KEY_BLOCK), F32))
        pad_scr[0:dec_seq, :] = vn_ref[0]
        acc_scr[...] = _dot(w.astype(BF16), pad_scr[...].astype(BF16))
        carry_scr[...] = inc

    q_rep = q_scr[...]
    lszs, parts = [], []
    for u in range(PAGES_PER_STEP):
        z2 = _dot(q_rep, k_refs[u][0].astype(BF16)) + bias_ref[...]
        sp = _softplus2(z2)
        lszs.append(z2 - sp)
        parts.append(jnp.concatenate(_hi_lo(sp), axis=1))
    cs = _dot(jnp.concatenate(parts, axis=0), cum_ref[...])
    carry = carry_scr[...]
    acc = acc_scr[...]
    for u in range(PAGES_PER_STEP):
        cs_u = cs[u * n_rows:(u + 1) * n_rows]
        w = jnp.exp2(lszs[u] - (cs_u[:, :KEY_BLOCK] + carry))
        acc = acc + _dot_nt(w.astype(BF16), v_refs[u][0].astype(BF16))
        carry = carry + cs_u[:, KEY_BLOCK:]
    carry_scr[...] = carry
    acc_scr[...] = acc

    @when(step == n_steps - 1)
    def _():
        sg = sg_ref[0]
        for t in range(dec_seq):
            blk = jnp.where(head_mask, acc_scr[t * N_HEADS:(t + 1) * N_HEADS, :], 0.0)
            o_ref[0, t:t + 1, :] = jnp.sum(blk, axis=0, keepdims=True) * sg[t:t + 1, :]


def _attn_sample(q, sg, bias_rows, k_new, v_new, cache_k, cache_v, page_table, cum, rider=None, ride_args=()):
    n_seq, dec_seq, _ = q.shape
    n_pages = page_table.shape[1]
    n_steps = n_pages // PAGES_PER_STEP
    n_rows = dec_seq * N_HEADS
    seq_spec = pl.BlockSpec((1, dec_seq, E_B), lambda b, j, pt: (b, 0, 0))

    ride_ops, ride_specs, ride_out_shape, ride_out_specs, ride_scratch = [], [], [], [], []
    if rider is not None:
        assert n_steps == 1 and n_seq % RIDE_STEPS == 0
        n_row_args = 1 if rider == "pre" else 3
        rows = ride_args[0].shape[0]
        tm = rows // (n_seq // RIDE_STEPS)
        tile_of = lambda b, j, pt: b // RIDE_STEPS
        row_spec = lambda width: pl.BlockSpec((tm, width), lambda b, j, pt: (tile_of(b, j, pt), 0))
        for a in ride_args[:n_row_args]:
            op, spec = _row_tiled(a, tm, tile_of)
            ride_ops.append(op)
            ride_specs.append(spec)
        for a in ride_args[n_row_args:]:
            op, spec = _weight(a)
            ride_ops.append(op)
            ride_specs.append(spec)
        if rider == "post_pre":
            ride_out_shape.append(jax.ShapeDtypeStruct((rows, D_MODEL), F32))
            ride_scratch.append(pltpu.VMEM((tm, D_MODEL), F32))
        ride_out_shape += [jax.ShapeDtypeStruct((rows, E_B), BF16), jax.ShapeDtypeStruct((rows, E_B), F32)]
        ride_out_specs = [row_spec(s.shape[1]) for s in ride_out_shape]

    def page_spec(u):
        return pl.BlockSpec((1, E_B, PAGE_SIZE),
                            lambda b, j, pt: (pt[b, n_pages - 1 - (j * PAGES_PER_STEP + u)], 0, 0))

    const = lambda a: pl.BlockSpec(a.shape, lambda b, j, pt: (0,) * a.ndim, pipeline_mode=pl.Buffered(1))
    page_specs = [page_spec(u) for u in range(PAGES_PER_STEP)]
    grid_spec = pltpu.PrefetchScalarGridSpec(
        num_scalar_prefetch=1,
        grid=(n_seq, n_steps),
        in_specs=([seq_spec, seq_spec, const(bias_rows), seq_spec, seq_spec, const(cum)] + page_specs + page_specs
                  + ride_specs),
        out_specs=[seq_spec] + ride_out_specs,
        scratch_shapes=[pltpu.VMEM((n_rows, E_B), BF16), pltpu.VMEM((n_rows, E_B), F32),
                        pltpu.VMEM((n_rows, KEY_BLOCK), F32), pltpu.VMEM((KEY_BLOCK, E_B), F32)] + ride_scratch,
    )
    return pl.pallas_call(
        functools.partial(_attn_sample_kernel, dec_seq=dec_seq, n_steps=n_steps, rider=rider),
        grid_spec=grid_spec,
        out_shape=[jax.ShapeDtypeStruct((n_seq, dec_seq, E_B), F32)] + ride_out_shape,
        compiler_params=pltpu.CompilerParams(dimension_semantics=("arbitrary", "arbitrary"),
                                             vmem_limit_bytes=VMEM_LIMIT_BYTES),
        name="attn_sample",
    )(page_table, q, sg, bias_rows, k_new, v_new, cum, *([cache_k] * PAGES_PER_STEP),
      *([cache_v] * PAGES_PER_STEP), *ride_ops)


def _row(a):
    return a.reshape(1, -1)


def _a_layers(x, p, wts, *, chunk, keep_chunk_state, kv_seq=None):
    (g_norm, w_in_a, ln_v_g, ln_v_b, ws_tiles, bs_tiles, w_out_a, g_kv, w_kv, w_ple, g_ple, w_ple_gate) = wts
    h = x
    chunk_rows = []
    k = v = kb = vb = None
    for i in range(N_A_LAYERS):
        last = i == N_A_LAYERS - 1
        outs = _a_layer(h, _Layer(p, i), _row(g_norm[i]), _Layer(w_in_a, i), _row(ln_v_g[i]), _row(ln_v_b[i]),
                        _Layer(ws_tiles, i), _Layer(bs_tiles, i), _Layer(w_out_a, i), _Layer(w_ple, i),
                        _row(g_ple[i]), _Layer(w_ple_gate, i),
                        _row(g_kv) if last else None, w_kv if last else None,
                        chunk=chunk, with_vn=keep_chunk_state, kv_seq=kv_seq if last else None)
        h = outs[0]
        pos = 1
        if keep_chunk_state:
            chunk_rows.append(outs[pos])
            pos += 1
        if last:
            k, v, kb, vb = outs[pos:pos + 4]
    return h, (k, v, kb, vb), chunk_rows


def kernel(x_prompt, x_sample, cache_k, cache_v, page_table, p_prompt, p_sample, g_norm, w_in_a, ln_v_g, ln_v_b,
           w_spatial, b_spatial, w_out_a, g_kv, w_kv, w_in_b, w_out_b, b_sb, w_ple, g_ple, w_ple_gate, g_final):
    batch, seq, _ = x_prompt.shape
    n_seq, dec_seq, _ = x_sample.shape
    n_pool = cache_k.shape[0]
    assert seq % ROW_TILE == 0 and (n_seq * dec_seq) % ROW_TILE == 0 and seq % Q_SUPER == 0
    assert (Q_SUPER // KEY_UNIT) % SCRATCH_SETS == 0 and page_table.shape[1] % PAGES_PER_STEP == 0
    assert CHUNK % dec_seq == 0 and dec_seq & (dec_seq - 1) == 0

    bf = lambda a: a.astype(BF16)
    cum = _cum_matrix()
    pages_t = lambda c: jnp.transpose(c, (0, 2, 3, 1)).reshape(n_pool, E_B, PAGE_SIZE)

    def weights(chunk):
        rep = CHUNK // chunk
        ws_tiles = jnp.tile(w_spatial[:, :, :chunk, :chunk], (1, 1, rep, rep))
        bs_tiles = jnp.broadcast_to(jnp.tile(b_spatial[:, :, :chunk], (1, 1, rep))[..., None],
                                    (N_A_LAYERS, N_GROUPS_A, CHUNK, LANES))
        return (g_norm, w_in_a_b, ln_v_g, ln_v_b, ws_tiles, bs_tiles, w_out_a_b, g_kv, w_kv_b, w_ple_b, g_ple,
                w_gate_b)

    w_in_a_b, w_out_a_b, w_kv_b, w_in_b_b, w_out_b_b, w_ple_b, w_gate_b = map(
        bf, (w_in_a, w_out_a, w_kv, w_in_b, w_out_b, w_ple, w_ple_gate))
    rows_p, rows_s = batch * seq, n_seq * dec_seq
    p_p = p_prompt.reshape(DEPTH, rows_p, PLE_DIM)
    p_s = p_sample.reshape(DEPTH, rows_s, PLE_DIM)

    h_p, (k_p, v_p, kb_p, vb_p), _ = _a_layers(x_prompt.reshape(rows_p, D_MODEL), p_p, weights(CHUNK), chunk=CHUNK,
                                               keep_chunk_state=False, kv_seq=seq)
    h_s, (k_s, v_s, _, _), chunk_rows = _a_layers(x_sample.reshape(rows_s, D_MODEL), p_s, weights(dec_seq),
                                                  chunk=dec_seq, keep_chunk_state=True)

    seq_rows = lambda a: a.reshape(n_seq, dec_seq, E_B)
    cache_kt, cache_vt = pages_t(cache_k), pages_t(cache_v)
    m_p = None
    post = lambda layer, jj: (_Layer(w_out_b_b, jj), _Layer(w_ple_b, layer), _row(g_ple[layer]),
                              _Layer(w_gate_b, layer))
    for j in range(DEPTH - N_A_LAYERS):
        i = N_A_LAYERS + j
        pre_args = (_row(g_norm[i]), _Layer(w_in_b_b, j))
        q_s, sg_s = _b_pre(h_s, *pre_args, F32)
        bias_rows = jnp.broadcast_to(jnp.tile(b_sb[j] * LOG2E, dec_seq)[:, None], (dec_seq * N_HEADS, KEY_BLOCK))
        if j == 0:
            rider, ride_args = "pre", (h_p,) + pre_args
        else:
            rider, ride_args = "post_pre", (h_p, m_p, _Layer(p_p, i - 1)) + post(i - 1, j - 1) + pre_args
        outs = _attn_sample(seq_rows(q_s), seq_rows(sg_s), bias_rows, seq_rows(k_s), seq_rows(v_s), cache_kt,
                            cache_vt, page_table, cum, rider, ride_args)
        m_s, (q_p, sg_p) = outs[0].reshape(rows_s, E_B), outs[-2:]
        if j > 0:
            h_p = outs[1]
        m_p = _attn_prompt(q_p, kb_p, vb_p, sg_p, b_sb[j] * LOG2E, cum, batch=batch, seq=seq)
        last = i == DEPTH - 1
        h_s = _b_post(h_s, m_s, _Layer(p_s, i), *post(i, j), _row(g_final), final_norm=last)
    y_s = h_s
    y_p = _b_post(h_p, m_p, _Layer(p_p, DEPTH - 1), *post(DEPTH - 1, DEPTH - N_A_LAYERS - 1), _row(g_final), final_norm=True)
    rows_last = lambda a: jnp.transpose(a.reshape(batch, N_HEADS, HEAD_DIM, seq), (0, 3, 1, 2))

    return (y_p.reshape(batch, seq, D_MODEL),
            y_s.reshape(n_seq, dec_seq, D_MODEL),
            rows_last(k_p),
            rows_last(v_p),
            k_s.reshape(n_seq, dec_seq, N_HEADS, HEAD_DIM),
            v_s.reshape(n_seq, dec_seq, N_HEADS, HEAD_DIM),
            jnp.stack(chunk_rows).reshape(N_A_LAYERS, n_seq, dec_seq, E_A))
```

```python
import functools
import math
from typing import NamedTuple

import jax
import jax.numpy as jnp
import numpy as np
from jax import lax
from jax.experimental import pallas as pl
from jax.experimental.pallas import tpu as pltpu

D_MODEL = 1024
DEPTH = 4
N_A_LAYERS = DEPTH // 2
CHUNK = 128
E_A = 2 * D_MODEL
N_GROUPS_A = 8
GROUP_A = E_A // N_GROUPS_A
N_HEADS = 16
HEAD_DIM = 64
E_B = N_HEADS * HEAD_DIM
PLE_DIM = 256
PAGE_SIZE = 128
EPS = 1e-6

LANES = 128
HEADS_PER_LANE_TILE = LANES // HEAD_DIM
KEY_BLOCK = 128
KEY_UNIT = 2 * KEY_BLOCK
Q_SUB = 128
Q_SUPER = 512
ROW_CHUNK = 32
SCRATCH_SETS = 2
BIAS_TERMS = 3
TILES_PER_STEP = 2
RIDE_STEPS = 2
ROW_TILE = 256
PAGES_PER_STEP = 16
VMEM_LIMIT_BYTES = 56 * 1024 * 1024

F32 = jnp.float32
BF16 = jnp.bfloat16
SIGN_BIT = np.int32(-2 ** 31)
LOG2E = math.log2(math.e)
Q_SCALE = HEAD_DIM ** -0.5 * LOG2E


def _dot(a, b):
    return jnp.dot(a, b, preferred_element_type=F32)


def _dot_nt(a, b):
    return lax.dot_general(a, b, (((1,), (1,)), ((), ())), preferred_element_type=F32)


def _rms(x, g):
    return x * lax.rsqrt(jnp.mean(x * x, axis=-1, keepdims=True) + EPS) * g


def _sigmoid(x):
    return 1.0 / (1.0 + jnp.exp(-x))


def _silu(x):
    return x * _sigmoid(x)


def _const_spec(shape):
    zeros = (0,) * len(shape)
    return pl.BlockSpec(shape, lambda *_: zeros, pipeline_mode=pl.Buffered(1))


class _Layer(NamedTuple):
    stack: jax.Array
    index: int


def _row_tiled(a, tm, tile_of=lambda i: i):
    if isinstance(a, _Layer):
        layer, width = a.index, a.stack.shape[2]
        return a.stack, pl.BlockSpec((None, tm, width), lambda *g: (layer, tile_of(*g), 0))
    return a, pl.BlockSpec((tm, a.shape[1]), lambda *g: (tile_of(*g), 0))


def _weight(w):
    if isinstance(w, _Layer):
        shape = w.stack.shape[1:]
        idx = (w.index,) + (0,) * len(shape)
        return w.stack, pl.BlockSpec((None,) + shape, lambda *_: idx, pipeline_mode=pl.Buffered(1))
    return w, _const_spec(w.shape)


def _ple(h1, p, wple_ref, gple_ref, wgate_ref):
    gate = _sigmoid(_dot(_rms(h1, gple_ref[...]).astype(BF16), wgate_ref[...]))
    return h1 + _dot(p.astype(BF16), wple_ref[...]) * gate


def _a_layer_kernel(*refs, chunk, with_vn, with_kv, kv_transposed):
    (h_ref, p_ref, gn_ref, win_ref, lng_ref, lnb_ref, ws_ref, bs_ref, wout_ref,
     wple_ref, gple_ref, wgate_ref) = refs[:12]
    pos = 12
    if with_kv:
        gkv_ref, wkv_ref = refs[pos:pos + 2]
        pos += 2
    h_out = refs[pos]
    pos += 1
    if with_vn:
        vn_out = refs[pos]
        pos += 1
    if with_kv:
        k_out, v_out, kb_out, vb_out = refs[pos:pos + 4]
        pos += 4
    vn_scr, m_scr = refs[pos:pos + 2]

    x = h_ref[...]
    xn = _rms(x, gn_ref[...]).astype(BF16)

    v = _dot(xn, win_ref[:, E_A:2 * E_A])
    mu = jnp.mean(v, axis=-1, keepdims=True)
    vc = v - mu
    vn = vc * lax.rsqrt(jnp.mean(vc * vc, axis=-1, keepdims=True) + EPS) * lng_ref[...] + lnb_ref[...]
    if with_vn:
        vn_out[...] = vn
    vn_scr[...] = vn.astype(BF16)

    row = lax.broadcasted_iota(jnp.int32, (CHUNK, CHUNK), 0)
    col = lax.broadcasted_iota(jnp.int32, (CHUNK, CHUNK), 1)
    shift = chunk.bit_length() - 1
    mix_mask = ((row >> shift) == (col >> shift)) & (col <= row)

    rows = x.shape[0]
    for g in range(N_GROUPS_A):
        lo, hi = g * GROUP_A, (g + 1) * GROUP_A
        wsm = jnp.where(mix_mask, ws_ref[g], 0.0).astype(BF16)
        bias = bs_ref[g]
        bias = jnp.concatenate([bias] * (GROUP_A // LANES), axis=1)
        u = _dot(xn, win_ref[:, lo:hi])
        gt = _dot(xn, win_ref[:, 2 * E_A + lo:2 * E_A + hi])
        ug = u * _silu(gt)
        for c in range(rows // CHUNK):
            r0, r1 = c * CHUNK, (c + 1) * CHUNK
            s = _dot(wsm, vn_scr[r0:r1, lo:hi]) + bias
            m_scr[r0:r1, lo:hi] = (ug[r0:r1] * s).astype(BF16)

    h1 = x + _dot(m_scr[...], wout_ref[...])
    h2 = _ple(h1, p_ref[...], wple_ref, gple_ref, wgate_ref)
    h_out[...] = h2
    if with_kv:
        kv = _dot(_rms(h2, gkv_ref[...]).astype(BF16), wkv_ref[...])
        k = kv[:, :E_B]
        vv = kv[:, E_B:]
        if kv_transposed:
            k_out[0] = k.T
            v_out[0] = vv.T
        else:
            k_out[...] = k
            v_out[...] = vv
        kb_out[...] = k.astype(BF16)
        vb_out[...] = vv.astype(BF16)


def _a_layer(h, p, gn, win, lng, lnb, ws, bs, wout, wple, gple, wgate, gkv=None, wkv=None, *,
             chunk, with_vn, kv_seq=None):
    rows = h.shape[0]
    with_kv = gkv is not None
    tm = ROW_TILE
    row_spec = lambda width: pl.BlockSpec((tm, width), lambda i: (i, 0))
    args = [h, p, gn, win, lng, lnb, ws, bs, wout, wple, gple, wgate]
    if with_kv:
        args += [gkv, wkv]
    weights, weight_specs = zip(*[_weight(a) for a in args[2:]])
    p_op, p_spec = _row_tiled(p, tm)
    args = [h, p_op] + list(weights)
    in_specs = [row_spec(D_MODEL), p_spec] + list(weight_specs)
    out_shape = [jax.ShapeDtypeStruct((rows, D_MODEL), F32)]
    out_specs = [row_spec(D_MODEL)]
    if with_vn:
        out_shape.append(jax.ShapeDtypeStruct((rows, E_A), F32))
        out_specs.append(row_spec(E_A))
    if with_kv:
        if kv_seq is None:
            out_shape += [jax.ShapeDtypeStruct((rows, E_B), F32)] * 2
            out_specs += [row_spec(E_B)] * 2
        else:
            per_seq = kv_seq // tm
            out_shape += [jax.ShapeDtypeStruct((rows // kv_seq, E_B, kv_seq), F32)] * 2
            out_specs += [pl.BlockSpec((1, E_B, tm), lambda i: (i // per_seq, 0, i % per_seq))] * 2
        out_shape += [jax.ShapeDtypeStruct((rows, E_B), BF16)] * 2
        out_specs += [row_spec(E_B)] * 2
    return pl.pallas_call(
        functools.partial(_a_layer_kernel, chunk=chunk, with_vn=with_vn, with_kv=with_kv,
                          kv_transposed=kv_seq is not None),
        grid=(rows // tm,),
        in_specs=in_specs,
        out_specs=out_specs,
        out_shape=out_shape,
        scratch_shapes=[pltpu.VMEM((tm, E_A), BF16), pltpu.VMEM((tm, E_A), BF16)],
        compiler_params=pltpu.CompilerParams(dimension_semantics=("arbitrary",),
                                             vmem_limit_bytes=VMEM_LIMIT_BYTES),
        name="a_layer",
    )(*args)


def _b_pre_rows(h, gn_ref, win_ref):
    xn = _rms(h, gn_ref[...]).astype(BF16)
    return _dot(xn, win_ref[:, :E_B]) * Q_SCALE, _silu(_dot(xn, win_ref[:, E_B:]))


def _b_pre_kernel(h_ref, gn_ref, win_ref, q_out, sg_out):
    q, sg = _b_pre_rows(h_ref[...], gn_ref, win_ref)
    q_out[...] = q.astype(q_out.dtype)
    sg_out[...] = sg


def _b_pre(h, gn, win, q_dtype):
    rows = h.shape[0]
    tm = ROW_TILE
    row_spec = pl.BlockSpec((tm, D_MODEL), lambda i: (i, 0))
    win_op, win_spec = _weight(win)
    return pl.pallas_call(
        _b_pre_kernel,
        grid=(rows // tm,),
        in_specs=[row_spec, _const_spec(gn.shape), win_spec],
        out_specs=[row_spec, row_spec],
        out_shape=[jax.ShapeDtypeStruct((rows, E_B), q_dtype), jax.ShapeDtypeStruct((rows, E_B), F32)],
        compiler_params=pltpu.CompilerParams(dimension_semantics=("arbitrary",),
                                             vmem_limit_bytes=VMEM_LIMIT_BYTES),
        name="b_pre",
    )(h, gn, win_op)


def _b_post_rows(h, m, p, wout_ref, wple_ref, gple_ref, wgate_ref):
    h1 = h + _dot(m.astype(BF16), wout_ref[...])
    return _ple(h1, p, wple_ref, gple_ref, wgate_ref)


def _b_post_kernel(h_ref, m_ref, p_ref, wout_ref, wple_ref, gple_ref, wgate_ref, gfin_ref, h_out, *,
                   final_norm):
    h2 = _b_post_rows(h_ref[...], m_ref[...], p_ref[...], wout_ref, wple_ref, gple_ref, wgate_ref)
    if final_norm:
        h2 = _rms(h2, gfin_ref[...])
    h_out[...] = h2


def _b_post(h, m, p, wout, wple, gple, wgate, gfin, *, final_norm):
    rows = h.shape[0]
    tm = ROW_TILE
    row_spec = lambda width: pl.BlockSpec((tm, width), lambda i: (i, 0))
    weights, weight_specs = zip(*[_weight(a) for a in (wout, wple, gple, wgate, gfin)])
    p_op, p_spec = _row_tiled(p, tm)
    return pl.pallas_call(
        functools.partial(_b_post_kernel, final_norm=final_norm),
        grid=(rows // tm,),
        in_specs=[row_spec(D_MODEL), row_spec(E_B), p_spec] + list(weight_specs),
        out_specs=row_spec(D_MODEL),
        out_shape=jax.ShapeDtypeStruct((rows, D_MODEL), F32),
        compiler_params=pltpu.CompilerParams(dimension_semantics=("arbitrary",),
                                             vmem_limit_bytes=VMEM_LIMIT_BYTES),
        name="b_post",
    )(h, m, p_op, *weights)


def _softplus2(z2):
    neg_abs = lax.bitcast_convert_type(lax.bitcast_convert_type(z2, jnp.int32) | SIGN_BIT, F32)
    return jnp.maximum(z2, 0.0) + jnp.log(1.0 + jnp.exp2(neg_abs)) * LOG2E


def _hi_lo(x):
    hi = x.astype(BF16)
    return hi, (x - hi.astype(F32)).astype(BF16)


def _sb_tile(z2, mask, cum_ref, carry):
    sp = _softplus2(z2)
    if mask is not None:
        sp = jnp.where(mask, sp, 0.0)
    cs = _dot(jnp.concatenate(_hi_lo(sp), axis=1), cum_ref[...])
    w = jnp.exp2(z2 - (cs[:, :KEY_BLOCK] + carry))
    if mask is not None:
        w = jnp.where(mask, w, 0.0)
    return w, cs[:, KEY_BLOCK:]


def _cum_matrix():
    j = jnp.arange(KEY_BLOCK)[:, None]
    s = jnp.arange(KEY_BLOCK)[None, :]
    lower = (j >= s).astype(BF16)
    half = jnp.concatenate([lower, jnp.ones((KEY_BLOCK, LANES), BF16)], axis=1)
    return jnp.concatenate([half, half], axis=0)


def _attn_prompt_kernel(q_ref, k_ref, v_ref, sg_ref, kbias_ref, cum_ref, o_ref,
                        qs_scr, z_scr, lcat_scr, cs_scr, w_scr, acc_scr, carry_scr):
    qi = pl.program_id(2)
    n_sub = Q_SUPER // Q_SUB
    m_rows = HEADS_PER_LANE_TILE * Q_SUPER
    units_per_super = Q_SUPER // KEY_UNIT
    halves = KEY_UNIT // KEY_BLOCK

    lane = lax.broadcasted_iota(jnp.int32, (Q_SUB, LANES), 1)
    first = lane < HEAD_DIM
    for tile in range(TILES_PER_STEP):
        for a in range(n_sub):
            q2 = q_ref[a * Q_SUB:(a + 1) * Q_SUB, tile * LANES:(tile + 1) * LANES]
            zero = jnp.zeros_like(q2)
            for e in range(HEADS_PER_LANE_TILE):
                rows = slice((2 * a + e) * Q_SUB, (2 * a + e + 1) * Q_SUB)
                qs_scr[tile, rows, 0:LANES] = jnp.where(first, q2, zero) if e == 0 else jnp.where(first, zero, q2)
                ones = (lane >= e * BIAS_TERMS) & (lane < (e + 1) * BIAS_TERMS)
                qs_scr[tile, rows, LANES:2 * LANES] = jnp.where(ones, 1.0, 0.0).astype(BF16)
    acc_scr[...] = jnp.zeros_like(acc_scr)
    carry_scr[...] = jnp.zeros_like(carry_scr)

    def unit(tile, ku, r0, diag, st):
        start = pl.multiple_of(ku * KEY_UNIT, KEY_UNIT)
        lanes = slice(tile * LANES, (tile + 1) * LANES)
        st = tile * SCRATCH_SETS + st
        keys = jnp.concatenate([k_ref[pl.ds(start, KEY_UNIT), lanes], kbias_ref[tile]], axis=1)
        z_scr[st, r0:, :] = _dot_nt(qs_scr[tile, r0:, :], keys)
        chunks = range(r0 // ROW_CHUNK, m_rows // ROW_CHUNK)

        def visible(c):
            a, t0 = (c * ROW_CHUNK) // (2 * Q_SUB), (c * ROW_CHUNK) % Q_SUB
            t = lax.broadcasted_iota(jnp.int32, (ROW_CHUNK, KEY_UNIT), 0) + (a * Q_SUB + t0)
            s = lax.broadcasted_iota(jnp.int32, (ROW_CHUNK, KEY_UNIT), 1) + diag * KEY_UNIT
            return s < t

        for c in chunks:
            rows = slice(c * ROW_CHUNK, (c + 1) * ROW_CHUNK)
            sp = _softplus2(z_scr[st, rows, :])
            if diag is not None:
                sp = jnp.where(visible(c), sp, 0.0)
            hi, lo = _hi_lo(sp)
            for h in range(halves):
                cols = slice(h * KEY_BLOCK, (h + 1) * KEY_BLOCK)
                lcat_scr[st, h, rows, 0:KEY_BLOCK] = hi[:, cols]
                lcat_scr[st, h, rows, KEY_BLOCK:2 * KEY_BLOCK] = lo[:, cols]
        for h in range(halves):
            cs_scr[st, h, r0:, :] = _dot(lcat_scr[st, h, r0:, :], cum_ref[...])
        for c in chunks:
            rows = slice(c * ROW_CHUNK, (c + 1) * ROW_CHUNK)
            carry = carry_scr[tile, rows, :]
            z2 = z_scr[st, rows, :]
            ws = [None] * halves
            for h in reversed(range(halves)):
                cs = cs_scr[st, h, rows, :]
                ws[h] = jnp.exp2(z2[:, h * KEY_BLOCK:(h + 1) * KEY_BLOCK] - (cs[:, :KEY_BLOCK] + carry))
                carry = carry + cs[:, KEY_BLOCK:]
            carry_scr[tile, rows, :] = carry
            w = jnp.concatenate(ws, axis=1)
            if diag is not None:
                w = jnp.where(visible(c), w, 0.0)
            w_scr[st, rows, :] = w.astype(BF16)
        acc_scr[tile, r0:, :] += _dot(w_scr[st, r0:, :], v_ref[pl.ds(start, KEY_UNIT), lanes])

    for d in reversed(range(units_per_super)):
        for tile in range(TILES_PER_STEP):
            unit(tile, qi * units_per_super + d, d * (KEY_UNIT // Q_SUB) * 2 * Q_SUB, d, d % SCRATCH_SETS)

    def body(i, c):
        for u in range(SCRATCH_SETS):
            for tile in range(TILES_PER_STEP):
                unit(tile, (qi - i) * units_per_super - 1 - u, 0, None, u)
        return c

    lax.fori_loop(0, qi * (units_per_super // SCRATCH_SETS), body, 0)

    for tile in range(TILES_PER_STEP):
        lanes = slice(tile * LANES, (tile + 1) * LANES)
        for a in range(n_sub):
            rows = slice(a * Q_SUB, (a + 1) * Q_SUB)
            o2 = jnp.where(first, acc_scr[tile, (2 * a) * Q_SUB:(2 * a + 1) * Q_SUB, :],
                           acc_scr[tile, (2 * a + 1) * Q_SUB:(2 * a + 2) * Q_SUB, :])
            o_ref[rows, lanes] = (o2 * sg_ref[rows, lanes]).astype(o_ref.dtype)


def _bias_columns(bias2):
    terms, rest = [], bias2
    for _ in range(BIAS_TERMS):
        t = rest.astype(BF16)
        terms.append(t)
        rest = rest - t.astype(F32)
    cols = jnp.stack(terms, axis=1).reshape(E_B // LANES, HEADS_PER_LANE_TILE * BIAS_TERMS)
    cols = jnp.pad(cols, ((0, 0), (0, LANES - HEADS_PER_LANE_TILE * BIAS_TERMS)))
    return jnp.broadcast_to(cols[:, None, :], (E_B // LANES, KEY_UNIT, LANES))


def _attn_prompt(q, kb, vb, sg, bias2, cum, *, batch, seq):
    nq = seq // Q_SUPER
    n_tiles = E_B // LANES
    m_rows = HEADS_PER_LANE_TILE * Q_SUPER
    halves = KEY_UNIT // KEY_BLOCK
    width = TILES_PER_STEP * LANES
    sets = TILES_PER_STEP * SCRATCH_SETS
    q_spec = pl.BlockSpec((Q_SUPER, width), lambda b, hp, qi: (b * nq + qi, hp))
    kv_spec = pl.BlockSpec((seq, width), lambda b, hp, qi: (b, hp))
    kbias_spec = pl.BlockSpec((TILES_PER_STEP, KEY_UNIT, LANES), lambda b, hp, qi: (hp, 0, 0))
    return pl.pallas_call(
        _attn_prompt_kernel,
        grid=(batch, n_tiles // TILES_PER_STEP, nq),
        in_specs=[q_spec, kv_spec, kv_spec, q_spec, kbias_spec, _const_spec(cum.shape)],
        out_specs=q_spec,
        out_shape=jax.ShapeDtypeStruct((batch * seq, E_B), BF16),
        scratch_shapes=[pltpu.VMEM((TILES_PER_STEP, m_rows, 2 * LANES), BF16),
                        pltpu.VMEM((sets, m_rows, KEY_UNIT), F32),
                        pltpu.VMEM((sets, halves, m_rows, 2 * KEY_BLOCK), BF16),
                        pltpu.VMEM((sets, halves, m_rows, KEY_BLOCK + LANES), F32),
                        pltpu.VMEM((sets, m_rows, KEY_UNIT), BF16),
                        pltpu.VMEM((TILES_PER_STEP, m_rows, LANES), F32),
                        pltpu.VMEM((TILES_PER_STEP, m_rows, LANES), F32)],
        compiler_params=pltpu.CompilerParams(dimension_semantics=("arbitrary",) * 3,
                                             vmem_limit_bytes=VMEM_LIMIT_BYTES),
        name="attn_prompt",
    )(q, kb, vb, sg, _bias_columns(bias2), cum)


_RIDER_ARITY = {None: (0, 0), "pre": (3, 2), "post_pre": (9, 3)}


def _ride(kind, ins, outs, h_scr, step):
    first = step % RIDE_STEPS == 0
    if kind == "pre":
        h_ref, gn_ref, win_ref = ins
        q_out, sg_out = outs

        @pl.when(first)
        def _():
            xn = _rms(h_ref[...], gn_ref[...]).astype(BF16)
            q_out[...] = (_dot(xn, win_ref[:, :E_B]) * Q_SCALE).astype(q_out.dtype)

        @pl.when(jnp.logical_not(first))
        def _():
            xn = _rms(h_ref[...], gn_ref[...]).astype(BF16)
            sg_out[...] = _silu(_dot(xn, win_ref[:, E_B:]))
    else:
        h_ref, m_ref, p_ref, wout_ref, wple_ref, gple_ref, wgate_ref, gn_ref, win_ref = ins
        h_out, q_out, sg_out = outs

        @pl.when(first)
        def _():
            h_scr[...] = _b_post_rows(h_ref[...], m_ref[...], p_ref[...], wout_ref, wple_ref, gple_ref, wgate_ref)
            h_out[...] = h_scr[...]

        @pl.when(jnp.logical_not(first))
        def _():
            q, sg = _b_pre_rows(h_scr[...], gn_ref, win_ref)
            q_out[...] = q.astype(q_out.dtype)
            sg_out[...] = sg


def _attn_sample_kernel(*refs, dec_seq, n_steps, rider):
    pt_ref, q_ref, sg_ref, bias_ref, kn_ref, vn_ref, cum_ref = refs[:7]
    k_refs = refs[7:7 + PAGES_PER_STEP]
    v_refs = refs[7 + PAGES_PER_STEP:7 + 2 * PAGES_PER_STEP]
    n_ride_in, n_ride_out = _RIDER_ARITY[rider]
    pos = 7 + 2 * PAGES_PER_STEP
    ride_in = refs[pos:pos + n_ride_in]
    o_ref = refs[pos + n_ride_in]
    ride_out = refs[pos + n_ride_in + 1:pos + n_ride_in + 1 + n_ride_out]
    q_scr, acc_scr, carry_scr, pad_scr = refs[pos + n_ride_in + 1 + n_ride_out:][:4]
    h_scr = refs[-1] if rider == "post_pre" else None
    del pt_ref
    step = pl.program_id(1)
    n_rows = dec_seq * N_HEADS
    when = (lambda cond: (lambda f: f())) if n_steps == 1 else pl.when

    if rider is not None:
        _ride(rider, ride_in, ride_out, h_scr, pl.program_id(0))

    head = lax.broadcasted_iota(jnp.int32, (N_HEADS, E_B), 0)
    feat = lax.broadcasted_iota(jnp.int32, (N_HEADS, E_B), 1)
    head_mask = (feat >> (HEAD_DIM.bit_length() - 1)) == head

    @when(step == 0)
    def _():
        q = q_ref[0]
        for t in range(dec_seq):
            rep = jnp.broadcast_to(q[t:t + 1, :], (N_HEADS, E_B))
            q_scr[t * N_HEADS:(t + 1) * N_HEADS, :] = jnp.where(head_mask, rep, 0.0).astype(BF16)
        row = lax.broadcasted_iota(jnp.int32, (n_rows, KEY_BLOCK), 0)
        key = lax.broadcasted_iota(jnp.int32, (n_rows, KEY_BLOCK), 1)
        new_mask = key < (row >> (N_HEADS.bit_length() - 1))
        pad_scr[...] = jnp.zeros_like(pad_scr)
        pad_scr[0:dec_seq, :] = kn_ref[0]
        z2 = _dot_nt(q_scr[...], pad_scr[...].astype(BF16)) + bias_ref[...]
        w, inc = _sb_tile(z2, new_mask, cum_ref, jnp.zeros((n_rows, KEY_BLOCK), F32))
        pad_scr[0:dec_seq, :] = vn_ref[0]
        acc_scr[...] = _dot(w.astype(BF16), pad_scr[...].astype(BF16))
        carry_scr[...] = inc

    q_rep = q_scr[...]
    zs, parts = [], []
    for u in range(PAGES_PER_STEP):
        z2 = _dot(q_rep, k_refs[u][0].astype(BF16)) + bias_ref[...]
        zs.append(z2)
        parts.append(jnp.concatenate(_hi_lo(_softplus2(z2)), axis=1))
    cs = _dot(jnp.concatenate(parts, axis=0), cum_ref[...])
    carry = carry_scr[...]
    acc = acc_scr[...]
    for u in range(PAGES_PER_STEP):
        cs_u = cs[u * n_rows:(u + 1) * n_rows]
        w = jnp.exp2(zs[u] - (cs_u[:, :KEY_BLOCK] + carry))
        acc = acc + _dot_nt(w.astype(BF16), v_refs[u][0].astype(BF16))
        carry = carry + cs_u[:, KEY_BLOCK:]
    carry_scr[...] = carry
    acc_scr[...] = acc

    @when(step == n_steps - 1)
    def _():
        sg = sg_ref[0]
        for t in range(dec_seq):
            blk = jnp.where(head_mask, acc_scr[t * N_HEADS:(t + 1) * N_HEADS, :], 0.0)
            o_ref[0, t:t + 1, :] = jnp.sum(blk, axis=0, keepdims=True) * sg[t:t + 1, :]


def _attn_sample(q, sg, bias_rows, k_new, v_new, cache_k, cache_v, page_table, cum, rider=None, ride_args=()):
    n_seq, dec_seq, _ = q.shape
    n_pages = page_table.shape[1]
    n_steps = n_pages // PAGES_PER_STEP
    n_rows = dec_seq * N_HEADS
    seq_spec = pl.BlockSpec((1, dec_seq, E_B), lambda b, j, pt: (b, 0, 0))

    ride_ops, ride_specs, ride_out_shape, ride_out_specs, ride_scratch = [], [], [], [], []
    if rider is not None:
        assert n_steps == 1 and n_seq % RIDE_STEPS == 0
        n_row_args = 1 if rider == "pre" else 3
        rows = ride_args[0].shape[0]
        tm = rows // (n_seq // RIDE_STEPS)
        tile_of = lambda b, j, pt: b // RIDE_STEPS
        row_spec = lambda width: pl.BlockSpec((tm, width), lambda b, j, pt: (tile_of(b, j, pt), 0))
        for a in ride_args[:n_row_args]:
            op, spec = _row_tiled(a, tm, tile_of)
            ride_ops.append(op)
            ride_specs.append(spec)
        for a in ride_args[n_row_args:]:
            op, spec = _weight(a)
            ride_ops.append(op)
            ride_specs.append(spec)
        if rider == "post_pre":
            ride_out_shape.append(jax.ShapeDtypeStruct((rows, D_MODEL), F32))
            ride_scratch.append(pltpu.VMEM((tm, D_MODEL), F32))
        ride_out_shape += [jax.ShapeDtypeStruct((rows, E_B), BF16), jax.ShapeDtypeStruct((rows, E_B), F32)]
        ride_out_specs = [row_spec(s.shape[1]) for s in ride_out_shape]

    def page_spec(u):
        return pl.BlockSpec((1, E_B, PAGE_SIZE),
                            lambda b, j, pt: (pt[b, n_pages - 1 - (j * PAGES_PER_STEP + u)], 0, 0))

    const = lambda a: pl.BlockSpec(a.shape, lambda b, j, pt: (0,) * a.ndim, pipeline_mode=pl.Buffered(1))
    page_specs = [page_spec(u) for u in range(PAGES_PER_STEP)]
    grid_spec = pltpu.PrefetchScalarGridSpec(
        num_scalar_prefetch=1,
        grid=(n_seq, n_steps),
        in_specs=([seq_spec, seq_spec, const(bias_rows), seq_spec, seq_spec, const(cum)] + page_specs + page_specs
                  + ride_specs),
        out_specs=[seq_spec] + ride_out_specs,
        scratch_shapes=[pltpu.VMEM((n_rows, E_B), BF16), pltpu.VMEM((n_rows, E_B), F32),
                        pltpu.VMEM((n_rows, KEY_BLOCK), F32), pltpu.VMEM((KEY_BLOCK, E_B), F32)] + ride_scratch,
    )
    return pl.pallas_call(
        functools.partial(_attn_sample_kernel, dec_seq=dec_seq, n_steps=n_steps, rider=rider),
        grid_spec=grid_spec,
        out_shape=[jax.ShapeDtypeStruct((n_seq, dec_seq, E_B), F32)] + ride_out_shape,
        compiler_params=pltpu.CompilerParams(dimension_semantics=("arbitrary", "arbitrary"),
                                             vmem_limit_bytes=VMEM_LIMIT_BYTES),
        name="attn_sample",
    )(page_table, q, sg, bias_rows, k_new, v_new, cum, *([cache_k] * PAGES_PER_STEP),
      *([cache_v] * PAGES_PER_STEP), *ride_ops)


def _row(a):
    return a.reshape(1, -1)


def _a_layers(x, p, wts, *, chunk, keep_chunk_state, kv_seq=None):
    (g_norm, w_in_a, ln_v_g, ln_v_b, ws_tiles, bs_tiles, w_out_a, g_kv, w_kv, w_ple, g_ple, w_ple_gate) = wts
    h = x
    chunk_rows = []
    k = v = kb = vb = None
    for i in range(N_A_LAYERS):
        last = i == N_A_LAYERS - 1
        outs = _a_layer(h, _Layer(p, i), _row(g_norm[i]), _Layer(w_in_a, i), _row(ln_v_g[i]), _row(ln_v_b[i]),
                        _Layer(ws_tiles, i), _Layer(bs_tiles, i), _Layer(w_out_a, i), _Layer(w_ple, i),
                        _row(g_ple[i]), _Layer(w_ple_gate, i),
                        _row(g_kv) if last else None, w_kv if last else None,
                        chunk=chunk, with_vn=keep_chunk_state, kv_seq=kv_seq if last else None)
        h = outs[0]
        pos = 1
        if keep_chunk_state:
            chunk_rows.append(outs[pos])
            pos += 1
        if last:
            k, v, kb, vb = outs[pos:pos + 4]
    return h, (k, v, kb, vb), chunk_rows


def kernel(x_prompt, x_sample, cache_k, cache_v, page_table, p_prompt, p_sample, g_norm, w_in_a, ln_v_g, ln_v_b,
           w_spatial, b_spatial, w_out_a, g_kv, w_kv, w_in_b, w_out_b, b_sb, w_ple, g_ple, w_ple_gate, g_final):
    batch, seq, _ = x_prompt.shape
    n_seq, dec_seq, _ = x_sample.shape
    n_pool = cache_k.shape[0]
    assert seq % ROW_TILE == 0 and (n_seq * dec_seq) % ROW_TILE == 0 and seq % Q_SUPER == 0
    assert (Q_SUPER // KEY_UNIT) % SCRATCH_SETS == 0 and page_table.shape[1] % PAGES_PER_STEP == 0
    assert CHUNK % dec_seq == 0 and dec_seq & (dec_seq - 1) == 0

    bf = lambda a: a.astype(BF16)
    cum = _cum_matrix()
    pages_t = lambda c: jnp.transpose(c, (0, 2, 3, 1)).reshape(n_pool, E_B, PAGE_SIZE)

    def weights(chunk):
        rep = CHUNK // chunk
        ws_tiles = jnp.tile(w_spatial[:, :, :chunk, :chunk], (1, 1, rep, rep))
        bs_tiles = jnp.broadcast_to(jnp.tile(b_spatial[:, :, :chunk], (1, 1, rep))[..., None],
                                    (N_A_LAYERS, N_GROUPS_A, CHUNK, LANES))
        return (g_norm, w_in_a_b, ln_v_g, ln_v_b, ws_tiles, bs_tiles, w_out_a_b, g_kv, w_kv_b, w_ple_b, g_ple,
                w_gate_b)

    w_in_a_b, w_out_a_b, w_kv_b, w_in_b_b, w_out_b_b, w_ple_b, w_gate_b = map(
        bf, (w_in_a, w_out_a, w_kv, w_in_b, w_out_b, w_ple, w_ple_gate))
    rows_p, rows_s = batch * seq, n_seq * dec_seq
    p_p = p_prompt.reshape(DEPTH, rows_p, PLE_DIM)
    p_s = p_sample.reshape(DEPTH, rows_s, PLE_DIM)

    h_p, (k_p, v_p, kb_p, vb_p), _ = _a_layers(x_prompt.reshape(rows_p, D_MODEL), p_p, weights(CHUNK), chunk=CHUNK,
                                               keep_chunk_state=False, kv_seq=seq)
    h_s, (k_s, v_s, _, _), chunk_rows = _a_layers(x_sample.reshape(rows_s, D_MODEL), p_s, weights(dec_seq),
                                                  chunk=dec_seq, keep_chunk_state=True)

    seq_rows = lambda a: a.reshape(n_seq, dec_seq, E_B)
    cache_kt, cache_vt = pages_t(cache_k), pages_t(cache_v)
    m_p = None
    post = lambda layer, jj: (_Layer(w_out_b_b, jj), _Layer(w_ple_b, layer), _row(g_ple[layer]),
                              _Layer(w_gate_b, layer))
    for j in range(DEPTH - N_A_LAYERS):
        i = N_A_LAYERS + j
        pre_args = (_row(g_norm[i]), _Layer(w_in_b_b, j))
        q_s, sg_s = _b_pre(h_s, *pre_args, F32)
        bias_rows = jnp.broadcast_to(jnp.tile(b_sb[j] * LOG2E, dec_seq)[:, None], (dec_seq * N_HEADS, KEY_BLOCK))
        if j == 0:
            rider, ride_args = "pre", (h_p,) + pre_args
        else:
            rider, ride_args = "post_pre", (h_p, m_p, _Layer(p_p, i - 1)) + post(i - 1, j - 1) + pre_args
        outs = _attn_sample(seq_rows(q_s), seq_rows(sg_s), bias_rows, seq_rows(k_s), seq_rows(v_s), cache_kt,
                            cache_vt, page_table, cum, rider, ride_args)
        m_s, (q_p, sg_p) = outs[0].reshape(rows_s, E_B), outs[-2:]
        if j > 0:
            h_p = outs[1]
        m_p = _attn_prompt(q_p, kb_p, vb_p, sg_p, b_sb[j] * LOG2E, cum, batch=batch, seq=seq)
        last = i == DEPTH - 1
        h_s = _b_post(h_s, m_s, _Layer(p_s, i), *post(i, j), _row(g_final), final_norm=last)
    y_s = h_s
    y_p = _b_post(h_p, m_p, _Layer(p_p, DEPTH - 1), *post(DEPTH - 1, DEPTH - N_A_LAYERS - 1), _row(g_final), final_norm=True)
    rows_last = lambda a: jnp.transpose(a.reshape(batch, N_HEADS, HEAD_DIM, seq), (0, 3, 1, 2))

    return (y_p.reshape(batch, seq, D_MODEL),
            y_s.reshape(n_seq, dec_seq, D_MODEL),
            rows_last(k_p),
            rows_last(v_p),
            k_s.reshape(n_seq, dec_seq, N_HEADS, HEAD_DIM),
            v_s.reshape(n_seq, dec_seq, N_HEADS, HEAD_DIM),
            jnp.stack(chunk_rows).reshape(N_A_LAYERS, n_seq, dec_seq, E_A))
```

```python
import functools
import math
from typing import NamedTuple

import jax
import jax.numpy as jnp
import numpy as np
from jax import lax
from jax.experimental import pallas as pl
from jax.experimental.pallas import tpu as pltpu

D_MODEL = 1024
DEPTH = 4
N_A_LAYERS = DEPTH // 2
CHUNK = 128
E_A = 2 * D_MODEL
N_GROUPS_A = 8
GROUP_A = E_A // N_GROUPS_A
N_HEADS = 16
HEAD_DIM = 64
E_B = N_HEADS * HEAD_DIM
PLE_DIM = 256
PAGE_SIZE = 128
EPS = 1e-6

LANES = 128
HEADS_PER_LANE_TILE = LANES // HEAD_DIM
KEY_BLOCK = 128
KEY_UNIT = 2 * KEY_BLOCK
Q_SUB = 128
Q_SUPER = 1024
ROW_CHUNK = 32
SCRATCH_SETS = 4
BIAS_TERMS = 3
TILES_PER_STEP = 1
RIDE_STEPS = 2
ROW_TILE = 256
PAGES_PER_STEP = 16
VMEM_LIMIT_BYTES = 56 * 1024 * 1024

F32 = jnp.float32
BF16 = jnp.bfloat16
SIGN_BIT = np.int32(-2 ** 31)
LOG2E = math.log2(math.e)
Q_SCALE = HEAD_DIM ** -0.5 * LOG2E


def _dot(a, b):
    return jnp.dot(a, b, preferred_element_type=F32)


def _dot_nt(a, b):
    return lax.dot_general(a, b, (((1,), (1,)), ((), ())), preferred_element_type=F32)


def _rms(x, g):
    return x * lax.rsqrt(jnp.mean(x * x, axis=-1, keepdims=True) + EPS) * g


def _sigmoid(x):
    return 1.0 / (1.0 + jnp.exp(-x))


def _silu(x):
    return x * _sigmoid(x)


def _const_spec(shape):
    zeros = (0,) * len(shape)
    return pl.BlockSpec(shape, lambda *_: zeros, pipeline_mode=pl.Buffered(1))


class _Layer(NamedTuple):
    stack: jax.Array
    index: int


def _row_tiled(a, tm, tile_of=lambda i: i):
    if isinstance(a, _Layer):
        layer, width = a.index, a.stack.shape[2]
        return a.stack, pl.BlockSpec((None, tm, width), lambda *g: (layer, tile_of(*g), 0))
    return a, pl.BlockSpec((tm, a.shape[1]), lambda *g: (tile_of(*g), 0))


def _weight(w):
    if isinstance(w, _Layer):
        shape = w.stack.shape[1:]
        idx = (w.index,) + (0,) * len(shape)
        return w.stack, pl.BlockSpec((None,) + shape, lambda *_: idx, pipeline_mode=pl.Buffered(1))
    return w, _const_spec(w.shape)


def _ple(h1, p, wple_ref, gple_ref, wgate_ref):
    gate = _sigmoid(_dot(_rms(h1, gple_ref[...]).astype(BF16), wgate_ref[...]))
    return h1 + _dot(p.astype(BF16), wple_ref[...]) * gate


def _a_layer_kernel(*refs, chunk, with_vn, with_kv, kv_transposed):
    (h_ref, p_ref, gn_ref, win_ref, lng_ref, lnb_ref, ws_ref, bs_ref, wout_ref,
     wple_ref, gple_ref, wgate_ref) = refs[:12]
    pos = 12
    if with_kv:
        gkv_ref, wkv_ref = refs[pos:pos + 2]
        pos += 2
    h_out = refs[pos]
    pos += 1
    if with_vn:
        vn_out = refs[pos]
        pos += 1
    if with_kv:
        k_out, v_out, kb_out, vb_out = refs[pos:pos + 4]
        pos += 4
    vn_scr, m_scr = refs[pos:pos + 2]

    x = h_ref[...]
    xn = _rms(x, gn_ref[...]).astype(BF16)

    v = _dot(xn, win_ref[:, E_A:2 * E_A])
    mu = jnp.mean(v, axis=-1, keepdims=True)
    vc = v - mu
    vn = vc * lax.rsqrt(jnp.mean(vc * vc, axis=-1, keepdims=True) + EPS) * lng_ref[...] + lnb_ref[...]
    if with_vn:
        vn_out[...] = vn
    vn_scr[...] = vn.astype(BF16)

    row = lax.broadcasted_iota(jnp.int32, (CHUNK, CHUNK), 0)
    col = lax.broadcasted_iota(jnp.int32, (CHUNK, CHUNK), 1)
    shift = chunk.bit_length() - 1
    mix_mask = ((row >> shift) == (col >> shift)) & (col <= row)

    rows = x.shape[0]
    for g in range(N_GROUPS_A):
        lo, hi = g * GROUP_A, (g + 1) * GROUP_A
        wsm = jnp.where(mix_mask, ws_ref[g], 0.0).astype(BF16)
        bias = bs_ref[g]
        bias = jnp.concatenate([bias] * (GROUP_A // LANES), axis=1)
        u = _dot(xn, win_ref[:, lo:hi])
        gt = _dot(xn, win_ref[:, 2 * E_A + lo:2 * E_A + hi])
        ug = u * _silu(gt)
        for c in range(rows // CHUNK):
            r0, r1 = c * CHUNK, (c + 1) * CHUNK
            s = _dot(wsm, vn_scr[r0:r1, lo:hi]) + bias
            m_scr[r0:r1, lo:hi] = (ug[r0:r1] * s).astype(BF16)

    h1 = x + _dot(m_scr[...], wout_ref[...])
    h2 = _ple(h1, p_ref[...], wple_ref, gple_ref, wgate_ref)
    h_out[...] = h2
    if with_kv:
        kv = _dot(_rms(h2, gkv_ref[...]).astype(BF16), wkv_ref[...])
        k = kv[:, :E_B]
        vv = kv[:, E_B:]
        if kv_transposed:
            k_out[0] = k.T
            v_out[0] = vv.T
        else:
            k_out[...] = k
            v_out[...] = vv
        kb_out[...] = k.astype(BF16)
        vb_out[...] = vv.astype(BF16)


def _a_layer(h, p, gn, win, lng, lnb, ws, bs, wout, wple, gple, wgate, gkv=None, wkv=None, *,
             chunk, with_vn, kv_seq=None):
    rows = h.shape[0]
    with_kv = gkv is not None
    tm = ROW_TILE
    row_spec = lambda width: pl.BlockSpec((tm, width), lambda i: (i, 0))
    args = [h, p, gn, win, lng, lnb, ws, bs, wout, wple, gple, wgate]
    if with_kv:
        args += [gkv, wkv]
    weights, weight_specs = zip(*[_weight(a) for a in args[2:]])
    p_op, p_spec = _row_tiled(p, tm)
    args = [h, p_op] + list(weights)
    in_specs = [row_spec(D_MODEL), p_spec] + list(weight_specs)
    out_shape = [jax.ShapeDtypeStruct((rows, D_MODEL), F32)]
    out_specs = [row_spec(D_MODEL)]
    if with_vn:
        out_shape.append(jax.ShapeDtypeStruct((rows, E_A), F32))
        out_specs.append(row_spec(E_A))
    if with_kv:
        if kv_seq is None:
            out_shape += [jax.ShapeDtypeStruct((rows, E_B), F32)] * 2
            out_specs += [row_spec(E_B)] * 2
        else:
            per_seq = kv_seq // tm
            out_shape += [jax.ShapeDtypeStruct((rows // kv_seq, E_B, kv_seq), F32)] * 2
            out_specs += [pl.BlockSpec((1, E_B, tm), lambda i: (i // per_seq, 0, i % per_seq))] * 2
        out_shape += [jax.ShapeDtypeStruct((rows, E_B), BF16)] * 2
        out_specs += [row_spec(E_B)] * 2
    return pl.pallas_call(
        functools.partial(_a_layer_kernel, chunk=chunk, with_vn=with_vn, with_kv=with_kv,
                          kv_transposed=kv_seq is not None),
        grid=(rows // tm,),
        in_specs=in_specs,
        out_specs=out_specs,
        out_shape=out_shape,
        scratch_shapes=[pltpu.VMEM((tm, E_A), BF16), pltpu.VMEM((tm, E_A), BF16)],
        compiler_params=pltpu.CompilerParams(dimension_semantics=("arbitrary",),
                                             vmem_limit_bytes=VMEM_LIMIT_BYTES),
        name="a_layer",
    )(*args)


def _b_pre_rows(h, gn_ref, win_ref):
    xn = _rms(h, gn_ref[...]).astype(BF16)
    return _dot(xn, win_ref[:, :E_B]) * Q_SCALE, _silu(_dot(xn, win_ref[:, E_B:]))


def _b_pre_kernel(h_ref, gn_ref, win_ref, q_out, sg_out):
    q, sg = _b_pre_rows(h_ref[...], gn_ref, win_ref)
    q_out[...] = q.astype(q_out.dtype)
    sg_out[...] = sg


def _b_pre(h, gn, win, q_dtype):
    rows = h.shape[0]
    tm = ROW_TILE
    row_spec = pl.BlockSpec((tm, D_MODEL), lambda i: (i, 0))
    win_op, win_spec = _weight(win)
    return pl.pallas_call(
        _b_pre_kernel,
        grid=(rows // tm,),
        in_specs=[row_spec, _const_spec(gn.shape), win_spec],
        out_specs=[row_spec, row_spec],
        out_shape=[jax.ShapeDtypeStruct((rows, E_B), q_dtype), jax.ShapeDtypeStruct((rows, E_B), F32)],
        compiler_params=pltpu.CompilerParams(dimension_semantics=("arbitrary",),
                                             vmem_limit_bytes=VMEM_LIMIT_BYTES),
        name="b_pre",
    )(h, gn, win_op)


def _b_post_rows(h, m, p, wout_ref, wple_ref, gple_ref, wgate_ref):
    h1 = h + _dot(m.astype(BF16), wout_ref[...])
    return _ple(h1, p, wple_ref, gple_ref, wgate_ref)


def _b_post_kernel(h_ref, m_ref, p_ref, wout_ref, wple_ref, gple_ref, wgate_ref, gfin_ref, h_out, *,
                   final_norm):
    h2 = _b_post_rows(h_ref[...], m_ref[...], p_ref[...], wout_ref, wple_ref, gple_ref, wgate_ref)
    if final_norm:
        h2 = _rms(h2, gfin_ref[...])
    h_out[...] = h2


def _b_post(h, m, p, wout, wple, gple, wgate, gfin, *, final_norm):
    rows = h.shape[0]
    tm = ROW_TILE
    row_spec = lambda width: pl.BlockSpec((tm, width), lambda i: (i, 0))
    weights, weight_specs = zip(*[_weight(a) for a in (wout, wple, gple, wgate, gfin)])
    p_op, p_spec = _row_tiled(p, tm)
    return pl.pallas_call(
        functools.partial(_b_post_kernel, final_norm=final_norm),
        grid=(rows // tm,),
        in_specs=[row_spec(D_MODEL), row_spec(E_B), p_spec] + list(weight_specs),
        out_specs=row_spec(D_MODEL),
        out_shape=jax.ShapeDtypeStruct((rows, D_MODEL), F32),
        compiler_params=pltpu.CompilerParams(dimension_semantics=("arbitrary",),
                                             vmem_limit_bytes=VMEM_LIMIT_BYTES),
        name="b_post",
    )(h, m, p_op, *weights)


def _softplus2(z2):
    neg_abs = lax.bitcast_convert_type(lax.bitcast_convert_type(z2, jnp.int32) | SIGN_BIT, F32)
    return jnp.maximum(z2, 0.0) + jnp.log(1.0 + jnp.exp2(neg_abs)) * LOG2E


def _hi_lo(x):
    hi = x.astype(BF16)
    return hi, (x - hi.astype(F32)).astype(BF16)


def _sb_tile(z2, mask, cum_ref, carry):
    sp = _softplus2(z2)
    if mask is not None:
        sp = jnp.where(mask, sp, 0.0)
    cs = _dot(jnp.concatenate(_hi_lo(sp), axis=1), cum_ref[...])
    w = jnp.exp2(z2 - (cs[:, :KEY_BLOCK] + carry))
    if mask is not None:
        w = jnp.where(mask, w, 0.0)
    return w, cs[:, KEY_BLOCK:]


def _cum_matrix():
    j = jnp.arange(KEY_BLOCK)[:, None]
    s = jnp.arange(KEY_BLOCK)[None, :]
    lower = (j >= s).astype(BF16)
    half = jnp.concatenate([lower, jnp.ones((KEY_BLOCK, LANES), BF16)], axis=1)
    return jnp.concatenate([half, half], axis=0)


def _attn_prompt_kernel(q_ref, k_ref, v_ref, sg_ref, kbias_ref, cum_ref, o_ref,
                        qs_scr, z_scr, lcat_scr, cs_scr, w_scr, acc_scr, carry_scr):
    qi = pl.program_id(2)
    n_sub = Q_SUPER // Q_SUB
    m_rows = HEADS_PER_LANE_TILE * Q_SUPER
    units_per_super = Q_SUPER // KEY_UNIT
    halves = KEY_UNIT // KEY_BLOCK

    lane = lax.broadcasted_iota(jnp.int32, (Q_SUB, LANES), 1)
    first = lane < HEAD_DIM
    for tile in range(TILES_PER_STEP):
        for a in range(n_sub):
            q2 = q_ref[a * Q_SUB:(a + 1) * Q_SUB, tile * LANES:(tile + 1) * LANES]
            zero = jnp.zeros_like(q2)
            for e in range(HEADS_PER_LANE_TILE):
                rows = slice((2 * a + e) * Q_SUB, (2 * a + e + 1) * Q_SUB)
                qs_scr[tile, rows, 0:LANES] = jnp.where(first, q2, zero) if e == 0 else jnp.where(first, zero, q2)
                ones = (lane >= e * BIAS_TERMS) & (lane < (e + 1) * BIAS_TERMS)
                qs_scr[tile, rows, LANES:2 * LANES] = jnp.where(ones, 1.0, 0.0).astype(BF16)
    acc_scr[...] = jnp.zeros_like(acc_scr)
    carry_scr[...] = jnp.zeros_like(carry_scr)

    def unit(tile, ku, r0, diag, st):
        start = pl.multiple_of(ku * KEY_UNIT, KEY_UNIT)
        lanes = slice(tile * LANES, (tile + 1) * LANES)
        st = tile * SCRATCH_SETS + st
        keys = jnp.concatenate([k_ref[pl.ds(start, KEY_UNIT), lanes], kbias_ref[tile]], axis=1)
        z_scr[st, r0:, :] = _dot_nt(qs_scr[tile, r0:, :], keys)
        chunks = range(r0 // ROW_CHUNK, m_rows // ROW_CHUNK)

        def visible(c):
            a, t0 = (c * ROW_CHUNK) // (2 * Q_SUB), (c * ROW_CHUNK) % Q_SUB
            t = lax.broadcasted_iota(jnp.int32, (ROW_CHUNK, KEY_UNIT), 0) + (a * Q_SUB + t0)
            s = lax.broadcasted_iota(jnp.int32, (ROW_CHUNK, KEY_UNIT), 1) + diag * KEY_UNIT
            return s < t

        for c in chunks:
            rows = slice(c * ROW_CHUNK, (c + 1) * ROW_CHUNK)
            sp = _softplus2(z_scr[st, rows, :])
            if diag is not None:
                sp = jnp.where(visible(c), sp, 0.0)
            hi, lo = _hi_lo(sp)
            for h in range(halves):
                cols = slice(h * KEY_BLOCK, (h + 1) * KEY_BLOCK)
                lcat_scr[st, h, rows, 0:KEY_BLOCK] = hi[:, cols]
                lcat_scr[st, h, rows, KEY_BLOCK:2 * KEY_BLOCK] = lo[:, cols]
        for h in range(halves):
            cs_scr[st, h, r0:, :] = _dot(lcat_scr[st, h, r0:, :], cum_ref[...])
        for c in chunks:
            rows = slice(c * ROW_CHUNK, (c + 1) * ROW_CHUNK)
            carry = carry_scr[tile, rows, :]
            z2 = z_scr[st, rows, :]
            ws = [None] * halves
            for h in reversed(range(halves)):
                cs = cs_scr[st, h, rows, :]
                ws[h] = jnp.exp2(z2[:, h * KEY_BLOCK:(h + 1) * KEY_BLOCK] - (cs[:, :KEY_BLOCK] + carry))
                carry = carry + cs[:, KEY_BLOCK:]
            carry_scr[tile, rows, :] = carry
            w = jnp.concatenate(ws, axis=1)
            if diag is not None:
                w = jnp.where(visible(c), w, 0.0)
            w_scr[st, rows, :] = w.astype(BF16)
        acc_scr[tile, r0:, :] += _dot(w_scr[st, r0:, :], v_ref[pl.ds(start, KEY_UNIT), lanes])

    for d in reversed(range(units_per_super)):
        for tile in range(TILES_PER_STEP):
            unit(tile, qi * units_per_super + d, d * (KEY_UNIT // Q_SUB) * 2 * Q_SUB, d, d % SCRATCH_SETS)

    def body(i, c):
        for u in range(SCRATCH_SETS):
            for tile in range(TILES_PER_STEP):
                unit(tile, (qi - i) * units_per_super - 1 - u, 0, None, u)
        return c

    lax.fori_loop(0, qi * (units_per_super // SCRATCH_SETS), body, 0)

    for tile in range(TILES_PER_STEP):
        lanes = slice(tile * LANES, (tile + 1) * LANES)
        for a in range(n_sub):
            rows = slice(a * Q_SUB, (a + 1) * Q_SUB)
            o2 = jnp.where(first, acc_scr[tile, (2 * a) * Q_SUB:(2 * a + 1) * Q_SUB, :],
                           acc_scr[tile, (2 * a + 1) * Q_SUB:(2 * a + 2) * Q_SUB, :])
            o_ref[rows, lanes] = (o2 * sg_ref[rows, lanes]).astype(o_ref.dtype)


def _bias_columns(bias2):
    terms, rest = [], bias2
    for _ in range(BIAS_TERMS):
        t = rest.astype(BF16)
        terms.append(t)
        rest = rest - t.astype(F32)
    cols = jnp.stack(terms, axis=1).reshape(E_B // LANES, HEADS_PER_LANE_TILE * BIAS_TERMS)
    cols = jnp.pad(cols, ((0, 0), (0, LANES - HEADS_PER_LANE_TILE * BIAS_TERMS)))
    return jnp.broadcast_to(cols[:, None, :], (E_B // LANES, KEY_UNIT, LANES))


def _attn_prompt(q, kb, vb, sg, bias2, cum, *, batch, seq):
    nq = seq // Q_SUPER
    n_tiles = E_B // LANES
    m_rows = HEADS_PER_LANE_TILE * Q_SUPER
    halves = KEY_UNIT // KEY_BLOCK
    width = TILES_PER_STEP * LANES
    sets = TILES_PER_STEP * SCRATCH_SETS
    q_spec = pl.BlockSpec((Q_SUPER, width), lambda b, hp, qi: (b * nq + qi, hp))
    kv_spec = pl.BlockSpec((seq, width), lambda b, hp, qi: (b, hp))
    kbias_spec = pl.BlockSpec((TILES_PER_STEP, KEY_UNIT, LANES), lambda b, hp, qi: (hp, 0, 0))
    return pl.pallas_call(
        _attn_prompt_kernel,
        grid=(batch, n_tiles // TILES_PER_STEP, nq),
        in_specs=[q_spec, kv_spec, kv_spec, q_spec, kbias_spec, _const_spec(cum.shape)],
        out_specs=q_spec,
        out_shape=jax.ShapeDtypeStruct((batch * seq, E_B), BF16),
        scratch_shapes=[pltpu.VMEM((TILES_PER_STEP, m_rows, 2 * LANES), BF16),
                        pltpu.VMEM((sets, m_rows, KEY_UNIT), F32),
                        pltpu.VMEM((sets, halves, m_rows, 2 * KEY_BLOCK), BF16),
                        pltpu.VMEM((sets, halves, m_rows, KEY_BLOCK + LANES), F32),
                        pltpu.VMEM((sets, m_rows, KEY_UNIT), BF16),
                        pltpu.VMEM((TILES_PER_STEP, m_rows, LANES), F32),
                        pltpu.VMEM((TILES_PER_STEP, m_rows, LANES), F32)],
        compiler_params=pltpu.CompilerParams(dimension_semantics=("arbitrary",) * 3,
                                             vmem_limit_bytes=VMEM_LIMIT_BYTES),
        name="attn_prompt",
    )(q, kb, vb, sg, _bias_columns(bias2), cum)


_RIDER_ARITY = {None: (0, 0), "pre": (3, 2), "post_pre": (9, 3)}


def _ride(kind, ins, outs, h_scr, step):
    first = step % RIDE_STEPS == 0
    if kind == "pre":
        h_ref, gn_ref, win_ref = ins
        q_out, sg_out = outs

        @pl.when(first)
        def _():
            xn = _rms(h_ref[...], gn_ref[...]).astype(BF16)
            q_out[...] = (_dot(xn, win_ref[:, :E_B]) * Q_SCALE).astype(q_out.dtype)

        @pl.when(jnp.logical_not(first))
        def _():
            xn = _rms(h_ref[...], gn_ref[...]).astype(BF16)
            sg_out[...] = _silu(_dot(xn, win_ref[:, E_B:]))
    else:
        h_ref, m_ref, p_ref, wout_ref, wple_ref, gple_ref, wgate_ref, gn_ref, win_ref = ins
        h_out, q_out, sg_out = outs

        @pl.when(first)
        def _():
            h_scr[...] = _b_post_rows(h_ref[...], m_ref[...], p_ref[...], wout_ref, wple_ref, gple_ref, wgate_ref)
            h_out[...] = h_scr[...]

        @pl.when(jnp.logical_not(first))
        def _():
            q, sg = _b_pre_rows(h_scr[...], gn_ref, win_ref)
            q_out[...] = q.astype(q_out.dtype)
            sg_out[...] = sg


def _attn_sample_kernel(*refs, dec_seq, n_steps, rider):
    pt_ref, q_ref, sg_ref, bias_ref, kn_ref, vn_ref, cum_ref = refs[:7]
    k_refs = refs[7:7 + PAGES_PER_STEP]
    v_refs = refs[7 + PAGES_PER_STEP:7 + 2 * PAGES_PER_STEP]
    n_ride_in, n_ride_out = _RIDER_ARITY[rider]
    pos = 7 + 2 * PAGES_PER_STEP
    ride_in = refs[pos:pos + n_ride_in]
    o_ref = refs[pos + n_ride_in]
    ride_out = refs[pos + n_ride_in + 1:pos + n_ride_in + 1 + n_ride_out]
    q_scr, acc_scr, carry_scr, pad_scr = refs[pos + n_ride_in + 1 + n_ride_out:][:4]
    h_scr = refs[-1] if rider == "post_pre" else None
    del pt_ref
    step = pl.program_id(1)
    n_rows = dec_seq * N_HEADS
    when = (lambda cond: (lambda f: f())) if n_steps == 1 else pl.when

    if rider is not None:
        _ride(rider, ride_in, ride_out, h_scr, pl.program_id(0))

    head = lax.broadcasted_iota(jnp.int32, (N_HEADS, E_B), 0)
    feat = lax.broadcasted_iota(jnp.int32, (N_HEADS, E_B), 1)
    head_mask = (feat >> (HEAD_DIM.bit_length() - 1)) == head

    @when(step == 0)
    def _():
        q = q_ref[0]
        for t in range(dec_seq):
            rep = jnp.broadcast_to(q[t:t + 1, :], (N_HEADS, E_B))
            q_scr[t * N_HEADS:(t + 1) * N_HEADS, :] = jnp.where(head_mask, rep, 0.0).astype(BF16)
        row = lax.broadcasted_iota(jnp.int32, (n_rows, KEY_BLOCK), 0)
        key = lax.broadcasted_iota(jnp.int32, (n_rows, KEY_BLOCK), 1)
        new_mask = key < (row >> (N_HEADS.bit_length() - 1))
        pad_scr[...] = jnp.zeros_like(pad_scr)
        pad_scr[0:dec_seq, :] = kn_ref[0]
        z2 = _dot_nt(q_scr[...], pad_scr[...].astype(BF16)) + bias_ref[...]
        w, inc = _sb_tile(z2, new_mask, cum_ref, jnp.zeros((n_rows, KEY_BLOCK), F32))
        pad_scr[0:dec_seq, :] = vn_ref[0]
        acc_scr[...] = _dot(w.astype(BF16), pad_scr[...].astype(BF16))
        carry_scr[...] = inc

    q_rep = q_scr[...]
    zs, parts = [], []
    for u in range(PAGES_PER_STEP):
        z2 = _dot(q_rep, k_refs[u][0].astype(BF16)) + bias_ref[...]
        zs.append(z2)
        parts.append(jnp.concatenate(_hi_lo(_softplus2(z2)), axis=1))
    cs = _dot(jnp.concatenate(parts, axis=0), cum_ref[...])
    carry = carry_scr[...]
    acc = acc_scr[...]
    for u in range(PAGES_PER_STEP):
        cs_u = cs[u * n_rows:(u + 1) * n_rows]
        w = jnp.exp2(zs[u] - (cs_u[:, :KEY_BLOCK] + carry))
        acc = acc + _dot_nt(w.astype(BF16), v_refs[u][0].astype(BF16))
        carry = carry + cs_u[:, KEY_BLOCK:]
    carry_scr[...] = carry
    acc_scr[...] = acc

    @when(step == n_steps - 1)
    def _():
        sg = sg_ref[0]
        for t in range(dec_seq):
            blk = jnp.where(head_mask, acc_scr[t * N_HEADS:(t + 1) * N_HEADS, :], 0.0)
            o_ref[0, t:t + 1, :] = jnp.sum(blk, axis=0, keepdims=True) * sg[t:t + 1, :]


def _attn_sample(q, sg, bias_rows, k_new, v_new, cache_k, cache_v, page_table, cum, rider=None, ride_args=()):
    n_seq, dec_seq, _ = q.shape
    n_pages = page_table.shape[1]
    n_steps = n_pages // PAGES_PER_STEP
    n_rows = dec_seq * N_HEADS
    seq_spec = pl.BlockSpec((1, dec_seq, E_B), lambda b, j, pt: (b, 0, 0))

    ride_ops, ride_specs, ride_out_shape, ride_out_specs, ride_scratch = [], [], [], [], []
    if rider is not None:
        assert n_steps == 1 and n_seq % RIDE_STEPS == 0
        n_row_args = 1 if rider == "pre" else 3
        rows = ride_args[0].shape[0]
        tm = rows // (n_seq // RIDE_STEPS)
        tile_of = lambda b, j, pt: b // RIDE_STEPS
        row_spec = lambda width: pl.BlockSpec((tm, width), lambda b, j, pt: (tile_of(b, j, pt), 0))
        for a in ride_args[:n_row_args]:
            op, spec = _row_tiled(a, tm, tile_of)
            ride_ops.append(op)
            ride_specs.append(spec)
        for a in ride_args[n_row_args:]:
            op, spec = _weight(a)
            ride_ops.append(op)
            ride_specs.append(spec)
        if rider == "post_pre":
            ride_out_shape.append(jax.ShapeDtypeStruct((rows, D_MODEL), F32))
            ride_scratch.append(pltpu.VMEM((tm, D_MODEL), F32))
        ride_out_shape += [jax.ShapeDtypeStruct((rows, E_B), BF16), jax.ShapeDtypeStruct((rows, E_B), F32)]
        ride_out_specs = [row_spec(s.shape[1]) for s in ride_out_shape]

    def page_spec(u):
        return pl.BlockSpec((1, E_B, PAGE_SIZE),
                            lambda b, j, pt: (pt[b, n_pages - 1 - (j * PAGES_PER_STEP + u)], 0, 0))

    const = lambda a: pl.BlockSpec(a.shape, lambda b, j, pt: (0,) * a.ndim, pipeline_mode=pl.Buffered(1))
    page_specs = [page_spec(u) for u in range(PAGES_PER_STEP)]
    grid_spec = pltpu.PrefetchScalarGridSpec(
        num_scalar_prefetch=1,
        grid=(n_seq, n_steps),
        in_specs=([seq_spec, seq_spec, const(bias_rows), seq_spec, seq_spec, const(cum)] + page_specs + page_specs
                  + ride_specs),
        out_specs=[seq_spec] + ride_out_specs,
        scratch_shapes=[pltpu.VMEM((n_rows, E_B), BF16), pltpu.VMEM((n_rows, E_B), F32),
                        pltpu.VMEM((n_rows, KEY_BLOCK), F32), pltpu.VMEM((KEY_BLOCK, E_B), F32)] + ride_scratch,
    )
    return pl.pallas_call(
        functools.partial(_attn_sample_kernel, dec_seq=dec_seq, n_steps=n_steps, rider=rider),
        grid_spec=grid_spec,
        out_shape=[jax.ShapeDtypeStruct((n_seq, dec_seq, E_B), F32)] + ride_out_shape,
        compiler_params=pltpu.CompilerParams(dimension_semantics=("arbitrary", "arbitrary"),
                                             vmem_limit_bytes=VMEM_LIMIT_BYTES),
        name="attn_sample",
    )(page_table, q, sg, bias_rows, k_new, v_new, cum, *([cache_k] * PAGES_PER_STEP),
      *([cache_v] * PAGES_PER_STEP), *ride_ops)


def _row(a):
    return a.reshape(1, -1)


def _a_layers(x, p, wts, *, chunk, keep_chunk_state, kv_seq=None):
    (g_norm, w_in_a, ln_v_g, ln_v_b, ws_tiles, bs_tiles, w_out_a, g_kv, w_kv, w_ple, g_ple, w_ple_gate) = wts
    h = x
    chunk_rows = []
    k = v = kb = vb = None
    for i in range(N_A_LAYERS):
        last = i == N_A_LAYERS - 1
        outs = _a_layer(h, _Layer(p, i), _row(g_norm[i]), _Layer(w_in_a, i), _row(ln_v_g[i]), _row(ln_v_b[i]),
                        _Layer(ws_tiles, i), _Layer(bs_tiles, i), _Layer(w_out_a, i), _Layer(w_ple, i),
                        _row(g_ple[i]), _Layer(w_ple_gate, i),
                        _row(g_kv) if last else None, w_kv if last else None,
                        chunk=chunk, with_vn=keep_chunk_state, kv_seq=kv_seq if last else None)
        h = outs[0]
        pos = 1
        if keep_chunk_state:
            chunk_rows.append(outs[pos])
            pos += 1
        if last:
            k, v, kb, vb = outs[pos:pos + 4]
    return h, (k, v, kb, vb), chunk_rows


def kernel(x_prompt, x_sample, cache_k, cache_v, page_table, p_prompt, p_sample, g_norm, w_in_a, ln_v_g, ln_v_b,
           w_spatial, b_spatial, w_out_a, g_kv, w_kv, w_in_b, w_out_b, b_sb, w_ple, g_ple, w_ple_gate, g_final):
    batch, seq, _ = x_prompt.shape
    n_seq, dec_seq, _ = x_sample.shape
    n_pool = cache_k.shape[0]
    assert seq % ROW_TILE == 0 and (n_seq * dec_seq) % ROW_TILE == 0 and seq % Q_SUPER == 0
    assert (Q_SUPER // KEY_UNIT) % SCRATCH_SETS == 0 and page_table.shape[1] % PAGES_PER_STEP == 0
    assert CHUNK % dec_seq == 0 and dec_seq & (dec_seq - 1) == 0

    bf = lambda a: a.astype(BF16)
    cum = _cum_matrix()
    pages_t = lambda c: jnp.transpose(c, (0, 2, 3, 1)).reshape(n_pool, E_B, PAGE_SIZE)

    def weights(chunk):
        rep = CHUNK // chunk
        ws_tiles = jnp.tile(w_spatial[:, :, :chunk, :chunk], (1, 1, rep, rep))
        bs_tiles = jnp.broadcast_to(jnp.tile(b_spatial[:, :, :chunk], (1, 1, rep))[..., None],
                                    (N_A_LAYERS, N_GROUPS_A, CHUNK, LANES))
        return (g_norm, w_in_a_b, ln_v_g, ln_v_b, ws_tiles, bs_tiles, w_out_a_b, g_kv, w_kv_b, w_ple_b, g_ple,
                w_gate_b)

    w_in_a_b, w_out_a_b, w_kv_b, w_in_b_b, w_out_b_b, w_ple_b, w_gate_b = map(
        bf, (w_in_a, w_out_a, w_kv, w_in_b, w_out_b, w_ple, w_ple_gate))
    rows_p, rows_s = batch * seq, n_seq * dec_seq
    p_p = p_prompt.reshape(DEPTH, rows_p, PLE_DIM)
    p_s = p_sample.reshape(DEPTH, rows_s, PLE_DIM)

    h_p, (k_p, v_p, kb_p, vb_p), _ = _a_layers(x_prompt.reshape(rows_p, D_MODEL), p_p, weights(CHUNK), chunk=CHUNK,
                                               keep_chunk_state=False, kv_seq=seq)
    h_s, (k_s, v_s, _, _), chunk_rows = _a_layers(x_sample.reshape(rows_s, D_MODEL), p_s, weights(dec_seq),
                                                  chunk=dec_seq, keep_chunk_state=True)

    seq_rows = lambda a: a.reshape(n_seq, dec_seq, E_B)
    cache_kt, cache_vt = pages_t(cache_k), pages_t(cache_v)
    m_p = None
    post = lambda layer, jj: (_Layer(w_out_b_b, jj), _Layer(w_ple_b, layer), _row(g_ple[layer]),
                              _Layer(w_gate_b, layer))
    for j in range(DEPTH - N_A_LAYERS):
        i = N_A_LAYERS + j
        pre_args = (_row(g_norm[i]), _Layer(w_in_b_b, j))
        q_s, sg_s = _b_pre(h_s, *pre_args, F32)
        bias_rows = jnp.broadcast_to(jnp.tile(b_sb[j] * LOG2E, dec_seq)[:, None], (dec_seq * N_HEADS, KEY_BLOCK))
        if j == 0:
            rider, ride_args = "pre", (h_p,) + pre_args
        else:
            rider, ride_args = "post_pre", (h_p, m_p, _Layer(p_p, i - 1)) + post(i - 1, j - 1) + pre_args
        outs = _attn_sample(seq_rows(q_s), seq_rows(sg_s), bias_rows, seq_rows(k_s), seq_rows(v_s), cache_kt,
                            cache_vt, page_table, cum, rider, ride_args)
        m_s, (q_p, sg_p) = outs[0].reshape(rows_s, E_B), outs[-2:]
        if j > 0:
            h_p = outs[1]
        m_p = _attn_prompt(q_p, kb_p, vb_p, sg_p, b_sb[j] * LOG2E, cum, batch=batch, seq=seq)
        last = i == DEPTH - 1
        h_s = _b_post(h_s, m_s, _Layer(p_s, i), *post(i, j), _row(g_final), final_norm=last)
    y_s = h_s
    y_p = _b_post(h_p, m_p, _Layer(p_p, DEPTH - 1), *post(DEPTH - 1, DEPTH - N_A_LAYERS - 1), _row(g_final), final_norm=True)
    rows_last = lambda a: jnp.transpose(a.reshape(batch, N_HEADS, HEAD_DIM, seq), (0, 3, 1, 2))

    return (y_p.reshape(batch, seq, D_MODEL),
            y_s.reshape(n_seq, dec_seq, D_MODEL),
            rows_last(k_p),
            rows_last(v_p),
            k_s.reshape(n_seq, dec_seq, N_HEADS, HEAD_DIM),
            v_s.reshape(n_seq, dec_seq, N_HEADS, HEAD_DIM),
            jnp.stack(chunk_rows).reshape(N_A_LAYERS, n_seq, dec_seq, E_A))
```
